```python
import math
import jax
import jax.numpy as jnp
from jax import lax
import numpy as np

D_MODEL = 2048
BATCH = 2
SEQ = 4096
DEPTH = 4
DEC_BATCH = 128
DEC_SEQ = 8
PAST_LEN = 8192
PAGE_SIZE = 128

N_META = 16
D_MIX = D_MODEL
CONV_DIM = D_MIX // 4
CONV_WIDTH = 31
MLA_HEADS = 8
MLA_V_DIM = 128
MLA_DIM = MLA_HEADS * MLA_V_DIM
Q_LORA = D_MODEL // 4
KV_LORA = D_MODEL // 8
NOPE_DIM = 128
ROPE_DIM = 64
ROPE_THETA = 10000.0
MLA_SCALE = 1.0 / math.sqrt(NOPE_DIM + ROPE_DIM)
Q_BLOCK = 128
SSM_DIM = D_MIX - CONV_DIM - MLA_DIM
SSM_HEAD_DIM = 64
SSM_HEADS = SSM_DIM // SSM_HEAD_DIM
SSM_GROUPS = 2
SSM_STATE = 128
SSM_CONV = 4
SSM_CHUNK = 128
XBC_DIM = SSM_DIM + 2 * SSM_GROUPS * SSM_STATE
EPS = 1e-6
SPLITS = (2 * CONV_DIM, CONV_DIM, Q_LORA, KV_LORA, ROPE_DIM, MLA_DIM, SSM_DIM, XBC_DIM, SSM_HEADS)
D_IN = sum(SPLITS)

kernel_name = 'hymba_conv_mla_ssd_decoder_step'


def rmsnorm(x, g):
    xf = x.astype(jnp.float32)
    y = xf * lax.rsqrt(jnp.mean(xf * xf, axis=-1, keepdims=True) + EPS)
    return (y * g.astype(jnp.float32)).astype(x.dtype)


def layernorm(x, g, b):
    xf = x.astype(jnp.float32)
    mu = jnp.mean(xf, axis=-1, keepdims=True)
    xc = xf - mu
    var = jnp.mean(xc * xc, axis=-1, keepdims=True)
    y = xc * lax.rsqrt(var + EPS) * g.astype(jnp.float32) + b.astype(jnp.float32)
    return y.astype(x.dtype)


def causal_dwconv(xp, w, b):
    c = xp.shape[-1]
    y = lax.conv_general_dilated(xp, w[:, None, :].astype(xp.dtype), window_strides=(1,), padding='VALID',
                                 dimension_numbers=('NWC', 'WIO', 'NWC'), feature_group_count=c)
    return y + b.astype(xp.dtype)


def rope_angles(pos):
    inv = ROPE_THETA ** (-jnp.arange(0, ROPE_DIM, 2, dtype=jnp.float32) / ROPE_DIM)
    ang = pos.astype(jnp.float32)[:, None] * inv[None, :]
    return jnp.cos(ang), jnp.sin(ang)


def apply_rope(x, cos, sin):
    x1, x2 = jnp.split(x, 2, axis=-1)
    c = cos.astype(x.dtype)
    s = sin.astype(x.dtype)
    return jnp.concatenate([x1 * c - x2 * s, x1 * s + x2 * c], axis=-1)


def split_proj(u):
    idx = [int(i) for i in np.cumsum(SPLITS)[:-1]]
    return jnp.split(u, idx, axis=-1)


def conv_branch(glu_in, gate, left, lp):
    a, b = jnp.split(glu_in, 2, axis=-1)
    v = a * jax.nn.sigmoid(b)
    vp = jnp.concatenate([left.astype(v.dtype), v], axis=1)
    y = causal_dwconv(vp, lp['conv_dw_w'], lp['conv_dw_b'])
    y = jax.nn.silu(layernorm(y, lp['conv_ln_g'], lp['conv_ln_b']))
    y = y @ lp['conv_pw_w'] + lp['conv_pw_b']
    return y * jax.nn.silu(gate), vp[:, -(CONV_WIDTH - 1):]


def mla_qk(cq_raw, ckv_raw, kpe_raw, cos, sin, lp):
    cq = rmsnorm(cq_raw, lp['mla_qa_g'])
    q = jnp.einsum('nlr,rhd->nlhd', cq, lp['mla_wq_b'])
    q_nope, q_pe = q[..., :NOPE_DIM], q[..., NOPE_DIM:]
    q_pe = apply_rope(q_pe, cos[:, None], sin[:, None])
    w_uk = lp['mla_wkv_b'][..., :NOPE_DIM]
    q_lat = jnp.einsum('nlhd,rhd->nlhr', q_nope, w_uk)
    ckv = rmsnorm(ckv_raw, lp['mla_kva_g'])
    kpe = apply_rope(kpe_raw, cos, sin)
    return q_lat, q_pe, ckv, kpe


def mla_attend(q_lat, q_pe, q_pos, ckv, kpe, k_pos):
    s = jnp.einsum('nqhr,nkr->nhqk', q_lat, ckv) + jnp.einsum('nqhe,nke->nhqk', q_pe, kpe)
    s = s.astype(jnp.float32) * MLA_SCALE
    mask = (k_pos[None, :] <= q_pos[:, None])[None, None]
    p = jax.nn.softmax(jnp.where(mask, s, -jnp.inf), axis=-1).astype(ckv.dtype)
    return jnp.einsum('nhqk,nkr->nqhr', p, ckv)


def mla_out(o_lat, gate, lp):
    w_uv = lp['mla_wkv_b'][..., NOPE_DIM:]
    o = jnp.einsum('nlhr,rhd->nlhd', o_lat, w_uv)
    return o.reshape(o.shape[0], o.shape[1], MLA_DIM) * jax.nn.silu(gate)


def ssm_inputs(xbc, dt_raw, left, lp):
    xp = jnp.concatenate([left.astype(xbc.dtype), xbc], axis=1)
    u = jax.nn.silu(causal_dwconv(xp, lp['ssm_conv_w'], lp['ssm_conv_b']))
    xs, bm, cm = jnp.split(u, [SSM_DIM, SSM_DIM + SSM_GROUPS * SSM_STATE], axis=-1)
    n, L = xs.shape[0], xs.shape[1]
    rep = SSM_HEADS // SSM_GROUPS
    xs = xs.reshape(n, L, SSM_HEADS, SSM_HEAD_DIM)
    bm = jnp.repeat(bm.reshape(n, L, SSM_GROUPS, SSM_STATE), rep, axis=2)
    cm = jnp.repeat(cm.reshape(n, L, SSM_GROUPS, SSM_STATE), rep, axis=2)
    dt = jax.nn.softplus(dt_raw.astype(jnp.float32) + lp['ssm_dt_bias'].astype(jnp.float32))
    a = -jnp.exp(lp['ssm_A_log'].astype(jnp.float32))
    return xs, dt, a, bm, cm, xp[:, -(SSM_CONV - 1):]


def ssd_scan(x, dt, a_head, bm, cm, d_skip, h0, chunk):
    n, L, H, P = x.shape
    nc = L // chunk
    xc = x.reshape(n, nc, chunk, H, P)
    bc = bm.reshape(n, nc, chunk, H, SSM_STATE)
    cc = cm.reshape(n, nc, chunk, H, SSM_STATE)
    dtc = dt.reshape(n, nc, chunk, H)
    a_cum = jnp.cumsum(dtc * a_head, axis=2)
    seg = a_cum[:, :, :, None, :] - a_cum[:, :, None, :, :]
    causal = jnp.tril(jnp.ones((chunk, chunk), dtype=bool))[None, None, :, :, None]
    lmat = jnp.exp(jnp.where(causal, seg, -jnp.inf))
    scores = jnp.einsum('ncihs,ncjhs->ncijh', cc, bc) * (lmat * dtc[:, :, None, :, :]).astype(x.dtype)
    y_diag = jnp.einsum('ncijh,ncjhp->ncihp', scores, xc)
    decay_end = jnp.exp(a_cum[:, :, -1:, :] - a_cum)
    states = jnp.einsum('ncjhs,ncjh,ncjhp->nchps', bc, (decay_end * dtc).astype(x.dtype), xc).astype(jnp.float32)
    chunk_decay = jnp.exp(a_cum[:, :, -1, :])

    def step(h, inp):
        dec, st = inp
        return h * dec[:, :, None, None] + st, h

    h_last, h_prev = lax.scan(step, h0.astype(jnp.float32),
                              (jnp.moveaxis(chunk_decay, 1, 0), jnp.moveaxis(states, 1, 0)))
    h_prev = jnp.moveaxis(h_prev, 0, 1)
    y_off = jnp.einsum('ncihs,nchps,ncih->ncihp', cc.astype(jnp.float32), h_prev, jnp.exp(a_cum))
    y = (y_diag.astype(jnp.float32) + y_off).reshape(n, L, H, P) + x.astype(jnp.float32) * d_skip.astype(jnp.float32)[:, None]
    return y.astype(x.dtype), h_last


def ssm_out(y, z, lp):
    n, L = y.shape[0], y.shape[1]
    g = (y.reshape(n, L, SSM_DIM) * jax.nn.silu(z)).reshape(n, L, SSM_GROUPS, SSM_DIM // SSM_GROUPS)
    g = rmsnorm(g, lp['ssm_norm_g'].reshape(SSM_GROUPS, SSM_DIM // SSM_GROUPS))
    return g.reshape(n, L, SSM_DIM)


def prompt_layer(h, lp, cos, sin, pos):
    n, L, _ = h.shape
    u = rmsnorm(h, lp['norm_g']) @ lp['w_in']
    glu_in, c_gate, cq_raw, ckv_raw, kpe_raw, m_gate, z, xbc, dt_raw = split_proj(u)
    conv_y, conv_st = conv_branch(glu_in, c_gate, jnp.zeros((n, CONV_WIDTH - 1, CONV_DIM), h.dtype), lp)
    q_lat, q_pe, ckv, kpe = mla_qk(cq_raw, ckv_raw, kpe_raw, cos, sin, lp)
    o_meta = mla_attend(q_lat[:, :N_META], q_pe[:, :N_META], pos[:N_META], ckv[:, :N_META], kpe[:, :N_META], pos[:N_META])
    nb = (L - N_META) // Q_BLOCK
    qlb = q_lat[:, N_META:].reshape(n, nb, Q_BLOCK, MLA_HEADS, KV_LORA).swapaxes(0, 1)
    qpb = q_pe[:, N_META:].reshape(n, nb, Q_BLOCK, MLA_HEADS, ROPE_DIM).swapaxes(0, 1)
    ppb = pos[N_META:].reshape(nb, Q_BLOCK)

    def blk(args):
        ql, qp, qpos = args
        return mla_attend(ql, qp, qpos, ckv, kpe, pos)

    o_real = lax.map(blk, (qlb, qpb, ppb)).swapaxes(0, 1).reshape(n, L - N_META, MLA_HEADS, KV_LORA)
    mla_y = mla_out(jnp.concatenate([o_meta, o_real], axis=1), m_gate, lp)
    xs, dt, a, bm, cm, ssm_conv_st = ssm_inputs(xbc, dt_raw, jnp.zeros((n, SSM_CONV - 1, XBC_DIM), h.dtype), lp)
    h0 = jnp.zeros((n, SSM_HEADS, SSM_HEAD_DIM, SSM_STATE), jnp.float32)
    y_m, h_m = ssd_scan(xs[:, :N_META], dt[:, :N_META], a, bm[:, :N_META], cm[:, :N_META], lp['ssm_D'], h0, N_META)
    y_r, h_r = ssd_scan(xs[:, N_META:], dt[:, N_META:], a, bm[:, N_META:], cm[:, N_META:], lp['ssm_D'], h_m, SSM_CHUNK)
    ssm_y = ssm_out(jnp.concatenate([y_m, y_r], axis=1), z, lp)
    mix = jnp.concatenate([conv_y, mla_y, ssm_y], axis=-1)
    h = h + mix @ lp['w_out']
    return h, (ckv, kpe, conv_st, ssm_conv_st, h_r.astype(h.dtype))


def sample_layer(h, lp, cos, sin, pos, past_ckv, past_kpe, conv_left, ssm_left, ssm_h):
    n, L, _ = h.shape
    u = rmsnorm(h, lp['norm_g']) @ lp['w_in']
    glu_in, c_gate, cq_raw, ckv_raw, kpe_raw, m_gate, z, xbc, dt_raw = split_proj(u)
    conv_y, conv_st = conv_branch(glu_in, c_gate, conv_left, lp)
    q_lat, q_pe, ckv, kpe = mla_qk(cq_raw, ckv_raw, kpe_raw, cos, sin, lp)
    keys_ckv = jnp.concatenate([past_ckv.astype(ckv.dtype), ckv], axis=1)
    keys_kpe = jnp.concatenate([past_kpe.astype(kpe.dtype), kpe], axis=1)
    k_pos = jnp.arange(keys_ckv.shape[1], dtype=jnp.int32)
    mla_y = mla_out(mla_attend(q_lat, q_pe, pos, keys_ckv, keys_kpe, k_pos), m_gate, lp)
    xs, dt, a, bm, cm, ssm_conv_st = ssm_inputs(xbc, dt_raw, ssm_left, lp)
    y_s, h_new = ssd_scan(xs, dt, a, bm, cm, lp['ssm_D'], ssm_h, L)
    ssm_y = ssm_out(y_s, z, lp)
    mix = jnp.concatenate([conv_y, mla_y, ssm_y], axis=-1)
    h = h + mix @ lp['w_out']
    return h, (ckv, kpe, conv_st, ssm_conv_st, h_new.astype(h.dtype))


def setup_inputs(seed: int = 0) -> dict:
    key = jax.random.key(seed)
    k = jax.random.split(key, 32)
    f32 = jnp.float32

    def nrm(kk, shape, scale):
        return jax.random.normal(kk, shape, f32) * scale

    n_pages = PAST_LEN // PAGE_SIZE
    n_used = DEC_BATCH * n_pages
    n_pool = n_used + n_used // 4
    x_prompt = nrm(k[0], (BATCH, SEQ, D_MODEL), 1.0)
    x_sample = nrm(k[1], (DEC_BATCH, DEC_SEQ, D_MODEL), 1.0)
    cache_ckv = nrm(k[2], (DEPTH, n_pool, PAGE_SIZE, KV_LORA), 1.0)
    cache_kpe = nrm(k[3], (DEPTH, n_pool, PAGE_SIZE, ROPE_DIM), 1.0)
    page_table = jax.random.permutation(k[4], n_pool)[:n_used].reshape(DEC_BATCH, n_pages).astype(jnp.int32)
    state_conv = nrm(k[5], (DEPTH, DEC_BATCH, CONV_WIDTH - 1, CONV_DIM), 0.5)
    state_ssm_conv = nrm(k[6], (DEPTH, DEC_BATCH, SSM_CONV - 1, XBC_DIM), 0.5)
    state_ssm = nrm(k[7], (DEPTH, DEC_BATCH, SSM_HEADS, SSM_HEAD_DIM, SSM_STATE), 0.3)
    meta_tokens = nrm(k[8], (N_META, D_MODEL), 1.0)
    norm_g = 1.0 + nrm(k[9], (DEPTH, D_MODEL), 0.02)
    w_in = nrm(k[10], (DEPTH, D_MODEL, D_IN), D_MODEL ** -0.5)
    conv_dw_w = nrm(k[11], (DEPTH, CONV_WIDTH, CONV_DIM), CONV_WIDTH ** -0.5)
    conv_dw_b = nrm(k[12], (DEPTH, CONV_DIM), 0.02)
    conv_ln_g = 1.0 + nrm(k[13], (DEPTH, CONV_DIM), 0.02)
    conv_ln_b = nrm(k[14], (DEPTH, CONV_DIM), 0.02)
    conv_pw_w = nrm(k[15], (DEPTH, CONV_DIM, CONV_DIM), CONV_DIM ** -0.5)
    conv_pw_b = nrm(k[16], (DEPTH, CONV_DIM), 0.02)
    mla_qa_g = 1.0 + nrm(k[17], (DEPTH, Q_LORA), 0.02)
    mla_wq_b = nrm(k[18], (DEPTH, Q_LORA, MLA_HEADS, NOPE_DIM + ROPE_DIM), Q_LORA ** -0.5)
    mla_kva_g = 1.0 + nrm(k[19], (DEPTH, KV_LORA), 0.02)
    mla_wkv_b = nrm(k[20], (DEPTH, KV_LORA, MLA_HEADS, NOPE_DIM + MLA_V_DIM), KV_LORA ** -0.5)
    ssm_conv_w = nrm(k[21], (DEPTH, SSM_CONV, XBC_DIM), SSM_CONV ** -0.5)
    ssm_conv_b = nrm(k[22], (DEPTH, XBC_DIM), 0.02)
    dt0 = jnp.exp(jax.random.uniform(k[23], (DEPTH, SSM_HEADS), f32, math.log(1e-3), math.log(1e-1)))
    ssm_dt_bias = dt0 + jnp.log(-jnp.expm1(-dt0))
    ssm_A_log = jnp.log(jax.random.uniform(k[24], (DEPTH, SSM_HEADS), f32, 1.0, 16.0))
    ssm_D = 1.0 + nrm(k[25], (DEPTH, SSM_HEADS), 0.02)
    ssm_norm_g = 1.0 + nrm(k[26], (DEPTH, SSM_DIM), 0.02)
    w_out = nrm(k[27], (DEPTH, D_MIX, D_MODEL), D_MIX ** -0.5)
    final_g = 1.0 + nrm(k[28], (D_MODEL,), 0.02)
    return {'x_prompt': x_prompt, 'x_sample': x_sample, 'cache_ckv': cache_ckv, 'cache_kpe': cache_kpe,
            'page_table': page_table, 'state_conv': state_conv, 'state_ssm_conv': state_ssm_conv,
            'state_ssm': state_ssm, 'meta_tokens': meta_tokens, 'norm_g': norm_g, 'w_in': w_in,
            'conv_dw_w': conv_dw_w, 'conv_dw_b': conv_dw_b, 'conv_ln_g': conv_ln_g, 'conv_ln_b': conv_ln_b,
            'conv_pw_w': conv_pw_w, 'conv_pw_b': conv_pw_b, 'mla_qa_g': mla_qa_g, 'mla_wq_b': mla_wq_b,
            'mla_kva_g': mla_kva_g, 'mla_wkv_b': mla_wkv_b, 'ssm_conv_w': ssm_conv_w, 'ssm_conv_b': ssm_conv_b,
            'ssm_dt_bias': ssm_dt_bias, 'ssm_A_log': ssm_A_log, 'ssm_D': ssm_D, 'ssm_norm_g': ssm_norm_g,
            'w_out': w_out, 'final_g': final_g}


def reference(x_prompt, x_sample, cache_ckv, cache_kpe, page_table, state_conv, state_ssm_conv, state_ssm,
              meta_tokens, norm_g, w_in, conv_dw_w, conv_dw_b, conv_ln_g, conv_ln_b, conv_pw_w, conv_pw_b,
              mla_qa_g, mla_wq_b, mla_kva_g, mla_wkv_b, ssm_conv_w, ssm_conv_b, ssm_dt_bias, ssm_A_log,
              ssm_D, ssm_norm_g, w_out, final_g):
    n_p = x_prompt.shape[0]
    n_s = x_sample.shape[0]
    hp = jnp.concatenate([jnp.broadcast_to(meta_tokens[None].astype(x_prompt.dtype), (n_p, N_META, D_MODEL)), x_prompt], axis=1)
    pos_p = jnp.arange(hp.shape[1], dtype=jnp.int32)
    cos_p, sin_p = rope_angles(pos_p)
    past_len = page_table.shape[1] * cache_ckv.shape[2]
    pos_s = past_len + jnp.arange(x_sample.shape[1], dtype=jnp.int32)
    cos_s, sin_s = rope_angles(pos_s)
    hs = x_sample
    st_p = [[] for _ in range(5)]
    st_s = [[] for _ in range(5)]
    for l in range(DEPTH):
        lp = {'norm_g': norm_g[l], 'w_in': w_in[l], 'conv_dw_w': conv_dw_w[l], 'conv_dw_b': conv_dw_b[l],
              'conv_ln_g': conv_ln_g[l], 'conv_ln_b': conv_ln_b[l], 'conv_pw_w': conv_pw_w[l],
              'conv_pw_b': conv_pw_b[l], 'mla_qa_g': mla_qa_g[l], 'mla_wq_b': mla_wq_b[l],
              'mla_kva_g': mla_kva_g[l], 'mla_wkv_b': mla_wkv_b[l], 'ssm_conv_w': ssm_conv_w[l],
              'ssm_conv_b': ssm_conv_b[l], 'ssm_dt_bias': ssm_dt_bias[l], 'ssm_A_log': ssm_A_log[l],
              'ssm_D': ssm_D[l], 'ssm_norm_g': ssm_norm_g[l], 'w_out': w_out[l]}
        hp, new_p = prompt_layer(hp, lp, cos_p, sin_p, pos_p)
        past_ckv = cache_ckv[l, page_table].reshape(n_s, past_len, KV_LORA)
        past_kpe = cache_kpe[l, page_table].reshape(n_s, past_len, ROPE_DIM)
        hs, new_s = sample_layer(hs, lp, cos_s, sin_s, pos_s, past_ckv, past_kpe,
                                 state_conv[l], state_ssm_conv[l], state_ssm[l])
        for i in range(5):
            st_p[i].append(new_p[i])
            st_s[i].append(new_s[i])
    y_prompt = rmsnorm(hp[:, N_META:], final_g)
    y_sample = rmsnorm(hs, final_g)
    ckv_prompt, kpe_prompt, conv_prompt, ssm_conv_prompt, ssm_prompt = [jnp.stack(s) for s in st_p]
    ckv_sample, kpe_sample, conv_sample, ssm_conv_sample, ssm_sample = [jnp.stack(s) for s in st_s]
    return (y_prompt, y_sample, ckv_prompt, kpe_prompt, conv_prompt, ssm_conv_prompt, ssm_prompt,
            ckv_sample, kpe_sample, conv_sample, ssm_conv_sample, ssm_sample)
```

```python
import functools
import math

import jax
import jax.numpy as jnp
from jax import lax
from jax.experimental import pallas as pl
from jax.experimental.pallas import tpu as pltpu

D_MODEL = 2048
DEPTH = 4
N_META = 16
CONV_DIM = 512
CONV_WIDTH = 31
MLA_HEADS = 8
MLA_V_DIM = 128
MLA_DIM = MLA_HEADS * MLA_V_DIM
Q_LORA = 512
KV_LORA = 256
NOPE_DIM = 128
ROPE_DIM = 64
ROPE_THETA = 10000.0
MLA_SCALE = 1.0 / math.sqrt(NOPE_DIM + ROPE_DIM)
SSM_DIM = 512
SSM_HEAD_DIM = 64
SSM_HEADS = 8
SSM_GROUPS = 2
SSM_STATE = 128
SSM_CONV = 4
XBC_DIM = SSM_DIM + 2 * SSM_GROUPS * SSM_STATE
EPS = 1e-6

LANE = 128
SUBLANE = 8
VMEM_LIMIT = 48 * 1024 * 1024

U_GLU = 0
U_MG = 1024
U_XBC = 2048
U_CG = 3072
U_CQ = 3584
U_Z = 4096
U_CKV = 4608
U_KPE = 4864
U_DT = 4992
U_TOTAL = 5120

QK_PAD = 384
CHUNK = 128
ATT_BLK = 384
PAGES_PER_STEP = 8

_BF = jnp.bfloat16
_F32 = jnp.float32


def _cparams(sem):
    return pltpu.CompilerParams(dimension_semantics=sem, vmem_limit_bytes=VMEM_LIMIT)


def _dot(a, b):
    return jnp.dot(a, b, preferred_element_type=_F32)


def _dot_nt(a, b):
    return lax.dot_general(a, b, (((1,), (1,)), ((), ())), preferred_element_type=_F32)


def _sigmoid(x):
    return 1.0 / (1.0 + jnp.exp(-x))


def _silu(x):
    return x * _sigmoid(x)


def _softplus(x):
    return jnp.maximum(x, 0.0) + jnp.log(1.0 + jnp.exp(-jnp.abs(x)))


def _exact_dot_01(mat01_bf16, x):
    hi = x.astype(_BF)
    r1 = x - hi.astype(_F32)
    mid = r1.astype(_BF)
    lo = (r1 - mid.astype(_F32)).astype(_BF)
    return _dot(mat01_bf16, hi) + _dot(mat01_bf16, mid) + _dot(mat01_bf16, lo)


def _inproj_kernel(x_ref, g_ref, w_ref, o_ref, xn_ref):
    @pl.when(pl.program_id(1) == 0)
    def _():
        x = x_ref[...]
        ms = jnp.mean(x * x, axis=-1, keepdims=True)
        xn_ref[...] = (x * lax.rsqrt(ms + EPS) * g_ref[...]).astype(_BF)

    o_ref[...] = _dot(xn_ref[...], w_ref[...])


def _inproj(h, g, w, tm, tn=1024):
    t = h.shape[0]
    n = w.shape[1]
    return pl.pallas_call(
        _inproj_kernel,
        grid=(t // tm, n // tn),
        in_specs=[
            pl.BlockSpec((tm, D_MODEL), lambda i, j: (i, 0)),
            pl.BlockSpec((1, D_MODEL), lambda i, j: (0, 0)),
            pl.BlockSpec((D_MODEL, tn), lambda i, j: (0, j)),
        ],
        out_specs=pl.BlockSpec((tm, tn), lambda i, j: (i, j)),
        out_shape=jax.ShapeDtypeStruct((t, n), _F32),
        scratch_shapes=[pltpu.VMEM((tm, D_MODEL), _BF)],
        compiler_params=_cparams(("parallel", "arbitrary")),
        name="inproj",
    )(h, g, w)


def _conv_tail(y, cg, lng_ref, lnb_ref, pww_ref, pwb_ref):
    mu = jnp.mean(y, axis=-1, keepdims=True)
    yc = y - mu
    var = jnp.mean(yc * yc, axis=-1, keepdims=True)
    yn = yc * lax.rsqrt(var + EPS) * lng_ref[...] + lnb_ref[...]
    s = _silu(yn)
    o = _dot(s.astype(_BF), pww_ref[...]) + pwb_ref[...]
    return o * _silu(cg)


def _conv_prompt_kernel(glu_ref, cg_ref, dww_ref, dwb_ref, lng_ref, lnb_ref, pww_ref, pwb_ref,
                        v_ref, cy_ref, vp_ref, y_ref):
    @pl.when(pl.program_id(1) == 0)
    def _():
        vp_ref[0:32, :] = jnp.zeros((32, CONV_DIM), _F32)

    v = glu_ref[:, :CONV_DIM] * _sigmoid(glu_ref[:, CONV_DIM:])
    v_ref[...] = v
    vp_ref[32:32 + CHUNK, :] = v
    for c in range(CONV_DIM // LANE):
        cs = slice(c * LANE, (c + 1) * LANE)
        acc = jnp.zeros((CHUNK, LANE), _F32)
        for k in range(CONV_WIDTH):
            acc = acc + dww_ref[k:k + 1, cs] * vp_ref[2 + k:2 + k + CHUNK, cs]
        y_ref[:, cs] = acc + dwb_ref[:, cs]
    cy = _conv_tail(y_ref[...], cg_ref[...], lng_ref, lnb_ref, pww_ref, pwb_ref)
    cy_ref[...] = cy.astype(cy_ref.dtype)
    vp_ref[0:32, :] = vp_ref[CHUNK:CHUNK + 32, :]


def _conv_prompt(u, lw, nb, lp):
    t = u.shape[0]
    nt = lp // CHUNK
    row = lambda b, i: b * nt + i
    full = lambda shape: pl.BlockSpec(shape, lambda b, i: (0,) * len(shape))
    return pl.pallas_call(
        _conv_prompt_kernel,
        grid=(nb, nt),
        in_specs=[
            pl.BlockSpec((CHUNK, 2 * CONV_DIM), lambda b, i: (row(b, i), U_GLU // (2 * CONV_DIM))),
            pl.BlockSpec((CHUNK, CONV_DIM), lambda b, i: (row(b, i), U_CG // CONV_DIM)),
            full((CONV_WIDTH, CONV_DIM)), full((1, CONV_DIM)), full((1, CONV_DIM)), full((1, CONV_DIM)),
            full((CONV_DIM, CONV_DIM)), full((1, CONV_DIM)),
        ],
        out_specs=[
            pl.BlockSpec((CHUNK, CONV_DIM), lambda b, i: (row(b, i), 0)),
            pl.BlockSpec((CHUNK, CONV_DIM), lambda b, i: (row(b, i), 0)),
        ],
        out_shape=[jax.ShapeDtypeStruct((t, CONV_DIM), _F32), jax.ShapeDtypeStruct((t, CONV_DIM), _BF)],
        scratch_shapes=[pltpu.VMEM((CHUNK + 32, CONV_DIM), _F32), pltpu.VMEM((CHUNK, CONV_DIM), _F32)],
        compiler_params=_cparams(("parallel", "arbitrary")),
        name="conv_prompt",
    )(u, u, lw["dww"], lw["dwb"], lw["lng"], lw["lnb"], lw["pww"], lw["pwb"])


REQ_BLK = CHUNK // SUBLANE


def _conv_sample_kernel(glu_ref, cg_ref, left_ref, dww_ref, dwb_ref, lng_ref, lnb_ref, pww_ref, pwb_ref,
                        st_ref, cy_ref, vp_ref, y_ref):
    v = glu_ref[:, :CONV_DIM] * _sigmoid(glu_ref[:, CONV_DIM:])
    vp_ref[:, 0:32, :] = left_ref[...]
    vp_ref[:, 32:40, :] = v.reshape(REQ_BLK, SUBLANE, CONV_DIM)
    for c in range(CONV_DIM // LANE):
        cs = slice(c * LANE, (c + 1) * LANE)
        acc = jnp.zeros((REQ_BLK, SUBLANE, LANE), _F32)
        for k in range(CONV_WIDTH):
            acc = acc + dww_ref[k:k + 1, cs][None] * vp_ref[:, 2 + k:2 + k + SUBLANE, cs]
        y_ref[:, :, cs] = acc + dwb_ref[:, cs][None]
    y = y_ref[...].reshape(CHUNK, CONV_DIM)
    cy = _conv_tail(y, cg_ref[...], lng_ref, lnb_ref, pww_ref, pwb_ref)
    cy_ref[...] = cy.astype(cy_ref.dtype)
    st_ref[...] = vp_ref[:, 8:40, :]


def _conv_sample(u, left32, lw):
    t = u.shape[0]
    nreq = t // SUBLANE
    full = lambda shape: pl.BlockSpec(shape, lambda i: (0,) * len(shape))
    return pl.pallas_call(
        _conv_sample_kernel,
        grid=(nreq // REQ_BLK,),
        in_specs=[
            pl.BlockSpec((CHUNK, 2 * CONV_DIM), lambda i: (i, U_GLU // (2 * CONV_DIM))),
            pl.BlockSpec((CHUNK, CONV_DIM), lambda i: (i, U_CG // CONV_DIM)),
            pl.BlockSpec((REQ_BLK, 32, CONV_DIM), lambda i: (i, 0, 0)),
            full((CONV_WIDTH, CONV_DIM)), full((1, CONV_DIM)), full((1, CONV_DIM)), full((1, CONV_DIM)),
            full((CONV_DIM, CONV_DIM)), full((1, CONV_DIM)),
        ],
        out_specs=[
            pl.BlockSpec((REQ_BLK, 32, CONV_DIM), lambda i: (i, 0, 0)),
            pl.BlockSpec((CHUNK, CONV_DIM), lambda i: (i, 0)),
        ],
        out_shape=[jax.ShapeDtypeStruct((nreq, 32, CONV_DIM), _F32), jax.ShapeDtypeStruct((t, CONV_DIM), _BF)],
        scratch_shapes=[pltpu.VMEM((REQ_BLK, 40, CONV_DIM), _F32), pltpu.VMEM((REQ_BLK, SUBLANE, CONV_DIM), _F32)],
        compiler_params=_cparams(("parallel",)),
        name="conv_sample",
    )(u, u, left32, lw["dww"], lw["dwb"], lw["lng"], lw["lnb"], lw["pww"], lw["pwb"])


def _rope_slot(slot, cs):
    r = slot * cs
    r = r + pltpu.roll(r, 64, axis=1)
    lane = lax.broadcasted_iota(jnp.int32, r.shape, 1)
    return jnp.where(lane < ROPE_DIM, r, 0.0)


def _qk_kernel(cq_ref, ckv_ref, kpe_ref, cs_ref, qag_ref, wq_ref, wuk_ref, kvg_ref,
               q_ref, kcat_ref, ckvo_ref, kpeo_ref):
    cq = cq_ref[...]
    ms = jnp.mean(cq * cq, axis=-1, keepdims=True)
    cqn = (cq * lax.rsqrt(ms + EPS) * qag_ref[...]).astype(_BF)
    q = _dot(cqn, wq_ref[...])
    cs = cs_ref[...]
    for h in range(MLA_HEADS):
        qn = q[:, h * NOPE_DIM:(h + 1) * NOPE_DIM].astype(_BF)
        ql = _dot(qn, wuk_ref[h])
        q_ref[h, :, 0:KV_LORA] = ql.astype(q_ref.dtype)
        slot = q[:, MLA_HEADS * NOPE_DIM + h * LANE:MLA_HEADS * NOPE_DIM + (h + 1) * LANE]
        q_ref[h, :, KV_LORA:QK_PAD] = _rope_slot(slot, cs).astype(q_ref.dtype)
    c = ckv_ref[...]
    ms = jnp.mean(c * c, axis=-1, keepdims=True)
    ckv = c * lax.rsqrt(ms + EPS) * kvg_ref[...]
    ckvo_ref[...] = ckv
    kcat_ref[:, 0:KV_LORA] = ckv.astype(kcat_ref.dtype)
    kr = _rope_slot(kpe_ref[...], cs)
    kcat_ref[:, KV_LORA:QK_PAD] = kr.astype(kcat_ref.dtype)
    kpeo_ref[...] = kr[:, :ROPE_DIM]


def _qk(u, cs, lw, tm, qdtype):
    t = u.shape[0]
    full = lambda shape: pl.BlockSpec(shape, lambda i: (0,) * len(shape))
    return pl.pallas_call(
        _qk_kernel,
        grid=(t // tm,),
        in_specs=[
            pl.BlockSpec((tm, Q_LORA), lambda i: (i, U_CQ // Q_LORA)),
            pl.BlockSpec((tm, KV_LORA), lambda i: (i, U_CKV // KV_LORA)),
            pl.BlockSpec((tm, LANE), lambda i: (i, U_KPE // LANE)),
            pl.BlockSpec((tm, LANE), lambda i: (i, 0)),
            full((1, Q_LORA)), full((Q_LORA, 2 * MLA_HEADS * NOPE_DIM)),
            full((MLA_HEADS, NOPE_DIM, KV_LORA)), full((1, KV_LORA)),
        ],
        out_specs=[
            pl.BlockSpec((MLA_HEADS, tm, QK_PAD), lambda i: (0, i, 0)),
            pl.BlockSpec((tm, QK_PAD), lambda i: (i, 0)),
            pl.BlockSpec((tm, KV_LORA), lambda i: (i, 0)),
            pl.BlockSpec((tm, ROPE_DIM), lambda i: (i, 0)),
        ],
        out_shape=[
            jax.ShapeDtypeStruct((MLA_HEADS, t, QK_PAD), qdtype),
            jax.ShapeDtypeStruct((t, QK_PAD), qdtype),
            jax.ShapeDtypeStruct((t, KV_LORA), _F32),
            jax.ShapeDtypeStruct((t, ROPE_DIM), _F32),
        ],
        compiler_params=_cparams(("parallel",)),
        name="qk",
    )(u, u, u, cs, lw["qag"], lw["wq"], lw["wuk"], lw["kvg"])


def _attn_prompt_kernel(qi_ref, ki_ref, q_ref, k_ref, mg_ref, wuv_ref, o_ref, m_ref, l_ref, acc_ref):
    s_idx = pl.program_id(1)
    qi = qi_ref[s_idx]
    ki = ki_ref[s_idx]
    rows = MLA_HEADS * ATT_BLK

    @pl.when(ki == 0)
    def _():
        m_ref[...] = jnp.full((rows, 1), -jnp.inf, _F32)
        l_ref[...] = jnp.zeros((rows, 1), _F32)
        acc_ref[...] = jnp.zeros((rows, KV_LORA), _F32)

    q = q_ref[...].reshape(rows, QK_PAD)
    k = k_ref[...]
    s = _dot_nt(q, k) * MLA_SCALE
    tok = lax.broadcasted_iota(jnp.int32, (MLA_HEADS, ATT_BLK, ATT_BLK), 1).reshape(rows, ATT_BLK)
    qpos = qi * ATT_BLK + tok
    kpos = ki * ATT_BLK + lax.broadcasted_iota(jnp.int32, s.shape, 1)
    s = jnp.where(kpos <= qpos, s, -jnp.inf)
    m_prev = m_ref[...]
    m_new = jnp.maximum(m_prev, jnp.max(s, axis=-1, keepdims=True))
    alpha = jnp.exp(m_prev - m_new)
    p = jnp.exp(s - m_new)
    l_ref[...] = alpha * l_ref[...] + jnp.sum(p, axis=-1, keepdims=True)
    acc_ref[...] = alpha * acc_ref[...] + _dot(p.astype(_BF), k[:, :KV_LORA])
    m_ref[...] = m_new

    @pl.when(ki == qi)
    def _():
        for h in range(MLA_HEADS):
            rs = slice(h * ATT_BLK, (h + 1) * ATT_BLK)
            o = acc_ref[rs, :] / l_ref[rs, :]
            y = _dot(o.astype(_BF), wuv_ref[h])
            g = mg_ref[:, h * MLA_V_DIM:(h + 1) * MLA_V_DIM]
            o_ref[:, h * MLA_V_DIM:(h + 1) * MLA_V_DIM] = (y * _silu(g)).astype(o_ref.dtype)


def _attn_prompt(q, kcat, u, wuv, nb, lp):
    t = kcat.shape[0]
    nq = lp // ATT_BLK
    qi_tab, ki_tab = [], []
    for a in range(nq):
        for b in range(a + 1):
            qi_tab.append(a)
            ki_tab.append(b)
    qi_tab = jnp.asarray(qi_tab, jnp.int32)
    ki_tab = jnp.asarray(ki_tab, jnp.int32)
    nsteps = int(qi_tab.shape[0])
    grid_spec = pltpu.PrefetchScalarGridSpec(
        num_scalar_prefetch=2,
        grid=(nb, nsteps),
        in_specs=[
            pl.BlockSpec((MLA_HEADS, ATT_BLK, QK_PAD), lambda b, s, qt, kt: (0, b * nq + qt[s], 0)),
            pl.BlockSpec((ATT_BLK, QK_PAD), lambda b, s, qt, kt: (b * nq + kt[s], 0)),
            pl.BlockSpec((ATT_BLK, MLA_DIM), lambda b, s, qt, kt: (b * nq + qt[s], U_MG // MLA_DIM)),
            pl.BlockSpec((MLA_HEADS, KV_LORA, MLA_V_DIM), lambda b, s, qt, kt: (0, 0, 0)),
        ],
        out_specs=pl.BlockSpec((ATT_BLK, MLA_DIM), lambda b, s, qt, kt: (b * nq + qt[s], 0)),
        scratch_shapes=[
            pltpu.VMEM((MLA_HEADS * ATT_BLK, 1), _F32),
            pltpu.VMEM((MLA_HEADS * ATT_BLK, 1), _F32),
            pltpu.VMEM((MLA_HEADS * ATT_BLK, KV_LORA), _F32),
        ],
    )
    return pl.pallas_call(
        _attn_prompt_kernel,
        grid_spec=grid_spec,
        out_shape=jax.ShapeDtypeStruct((t, MLA_DIM), _BF),
        compiler_params=_cparams(("parallel", "arbitrary")),
        name="attn_prompt",
    )(qi_tab, ki_tab, q, kcat, u, wuv)


def _attn_sample_kernel(pt_ref, q_ref, kn_ref, mg_ref, wuv_ref, *rest):
    npg = PAGES_PER_STEP
    ckv_refs = rest[:npg]
    kpe_refs = rest[npg:2 * npg]
    o_ref, m_ref, l_ref, acc_ref = rest[2 * npg:]
    j = pl.program_id(1)
    rows = MLA_HEADS * SUBLANE

    @pl.when(j == 0)
    def _():
        m_ref[...] = jnp.full((rows, 1), -jnp.inf, _F32)
        l_ref[...] = jnp.zeros((rows, 1), _F32)
        acc_ref[...] = jnp.zeros((rows, KV_LORA), _F32)

    q = q_ref[...].reshape(rows, QK_PAD)
    ql = q[:, :KV_LORA].astype(_BF)
    qp = q[:, KV_LORA:KV_LORA + ROPE_DIM].astype(_BF)

    def update(s, vals):
        m_prev = m_ref[...]
        m_new = jnp.maximum(m_prev, jnp.max(s, axis=-1, keepdims=True))
        alpha = jnp.exp(m_prev - m_new)
        p = jnp.exp(s - m_new)
        l_ref[...] = alpha * l_ref[...] + jnp.sum(p, axis=-1, keepdims=True)
        pv = None
        for cs, v in vals:
            d = _dot(p[:, cs].astype(_BF), v)
            pv = d if pv is None else pv + d
        acc_ref[...] = alpha * acc_ref[...] + pv
        m_ref[...] = m_new

    ks = [r[...].astype(_BF) for r in ckv_refs]
    ss = [_dot_nt(ql, ks[i]) + _dot_nt(qp, kpe_refs[i][...].astype(_BF)) for i in range(npg)]
    s = jnp.concatenate(ss, axis=1) * MLA_SCALE
    update(s, [(slice(i * LANE, (i + 1) * LANE), ks[i]) for i in range(npg)])

    @pl.when(j == pl.num_programs(1) - 1)
    def _():
        kn = jnp.concatenate([kn_ref[...], jnp.zeros((LANE - SUBLANE, QK_PAD), _F32)], axis=0).astype(_BF)
        sn = _dot_nt(q.astype(_BF), kn) * MLA_SCALE
        tq = lax.broadcasted_iota(jnp.int32, sn.shape, 0) & (SUBLANE - 1)
        tk = lax.broadcasted_iota(jnp.int32, sn.shape, 1)
        sn = jnp.where(tk <= tq, sn, -jnp.inf)
        update(sn, [(slice(0, LANE), kn[:, :KV_LORA])])
        o = (acc_ref[...] / l_ref[...]).astype(_BF)
        y = _dot(o, wuv_ref[...])
        mg = mg_ref[...]
        for h in range(MLA_HEADS):
            cs = slice(h * MLA_V_DIM, (h + 1) * MLA_V_DIM)
            yh = y[h * SUBLANE:(h + 1) * SUBLANE, cs]
            o_ref[:, cs] = (yh * _silu(mg[:, cs])).astype(o_ref.dtype)


def _attn_sample(q, kcat, u, wuv_all, cache_ckv, cache_kpe, pt_flat, layer, n_pages):
    t = kcat.shape[0]
    nreq = t // SUBLANE
    npg = PAGES_PER_STEP
    page_size = cache_ckv.shape[2]

    def page_map(i):
        return lambda b, j, pt: (layer, pt[b * n_pages + j * npg + i], 0, 0)

    in_specs = [
        pl.BlockSpec((MLA_HEADS, SUBLANE, QK_PAD), lambda b, j, pt: (0, b, 0)),
        pl.BlockSpec((SUBLANE, QK_PAD), lambda b, j, pt: (b, 0)),
        pl.BlockSpec((SUBLANE, MLA_DIM), lambda b, j, pt: (b, U_MG // MLA_DIM)),
        pl.BlockSpec((KV_LORA, MLA_DIM), lambda b, j, pt: (0, 0)),
    ]
    in_specs += [pl.BlockSpec((None, None, page_size, KV_LORA), page_map(i)) for i in range(npg)]
    in_specs += [pl.BlockSpec((None, None, page_size, ROPE_DIM), page_map(i)) for i in range(npg)]
    rows = MLA_HEADS * SUBLANE
    grid_spec = pltpu.PrefetchScalarGridSpec(
        num_scalar_prefetch=1,
        grid=(nreq, n_pages // npg),
        in_specs=in_specs,
        out_specs=pl.BlockSpec((SUBLANE, MLA_DIM), lambda b, j, pt: (b, 0)),
        scratch_shapes=[
            pltpu.VMEM((rows, 1), _F32), pltpu.VMEM((rows, 1), _F32), pltpu.VMEM((rows, KV_LORA), _F32),
        ],
    )
    return pl.pallas_call(
        _attn_sample_kernel,
        grid_spec=grid_spec,
        out_shape=jax.ShapeDtypeStruct((t, MLA_DIM), _F32),
        compiler_params=_cparams(("parallel", "arbitrary")),
        name="attn_sample",
    )(pt_flat, q, kcat, u, wuv_all, *([cache_ckv] * npg), *([cache_kpe] * npg))


def _ssd_chunk(xa_ref, dt, a_row, mask, tmat, hprev_fn, d_ref, y_ref, xw_ref, lastmat=None):
    da = dt * a_row
    acum = _exact_dot_01(tmat, da)
    atot = acum[CHUNK - 1:CHUNK, :] if lastmat is None else _exact_dot_01(lastmat, da)
    acum_t = acum.T
    dt_t = dt.T
    ea = jnp.exp(acum)
    w = jnp.exp(atot - acum) * dt
    heads_per_group = SSM_HEADS // SSM_GROUPS
    for g in range(SSM_GROUPS):
        bm = xa_ref[:, SSM_DIM + g * SSM_STATE:SSM_DIM + (g + 1) * SSM_STATE].astype(_BF)
        cm = xa_ref[:, SSM_DIM + (SSM_GROUPS + g) * SSM_STATE:SSM_DIM + (SSM_GROUPS + g + 1) * SSM_STATE].astype(_BF)
        gmat = _dot_nt(cm, bm)
        yo = hprev_fn(g, cm)
        for hh in range(heads_per_group):
            h = g * heads_per_group + hh
            hs = slice(h * SSM_HEAD_DIM, (h + 1) * SSM_HEAD_DIM)
            seg = acum[:, h:h + 1] - acum_t[h:h + 1, :]
            lmat = jnp.exp(jnp.where(mask, seg, -jnp.inf))
            sc = gmat * (lmat * dt_t[h:h + 1, :])
            xh = xa_ref[:, hs]
            yd = _dot(sc.astype(_BF), xh.astype(_BF))
            yoh = yo[:, hh * SSM_HEAD_DIM:(hh + 1) * SSM_HEAD_DIM] * ea[:, h:h + 1]
            y_ref[:, hs] = yd + yoh + xh * d_ref[:, hs]
            xw_ref[:, hs] = xh * w[:, h:h + 1]
    return acum, atot


def _ssd_gate_norm(y, z, ng_ref, o_ref):
    gated = y * _silu(z)
    gw = SSM_DIM // SSM_GROUPS
    for g in range(SSM_GROUPS):
        cs = slice(g * gw, (g + 1) * gw)
        gg = gated[:, cs]
        ms = jnp.mean(gg * gg, axis=-1, keepdims=True)
        o_ref[:, cs] = (gg * lax.rsqrt(ms + EPS) * ng_ref[:, cs]).astype(o_ref.dtype)


def _short_conv_silu(xp_ref, cw_ref, cb_ref, xa_ref, lead):
    for c in range(XBC_DIM // LANE):
        cs = slice(c * LANE, (c + 1) * LANE)
        acc = jnp.zeros((CHUNK, LANE), _F32) + cb_ref[:, cs]
        for k in range(SSM_CONV):
            off = lead - (SSM_CONV - 1) + k
            acc = acc + cw_ref[k:k + 1, cs] * xp_ref[off:off + CHUNK, cs]
        xa_ref[:, cs] = _silu(acc)


def _ssd_prompt_kernel(xbc_ref, z_ref, dt_ref, cw_ref, cb_ref, dtb_ref, a_ref, d_ref, ng_ref,
                       o_ref, hout_ref, xp_ref, xa_ref, y_ref, xw_ref, h_ref, *, l_valid):
    c = pl.program_id(1)

    @pl.when(c == 0)
    def _():
        xp_ref[0:SUBLANE, :] = jnp.zeros((SUBLANE, XBC_DIM), _F32)
        h_ref[...] = jnp.zeros(h_ref.shape, _F32)

    xp_ref[SUBLANE:SUBLANE + CHUNK, :] = xbc_ref[...]
    _short_conv_silu(xp_ref, cw_ref, cb_ref, xa_ref, SUBLANE)
    xp_ref[0:SUBLANE, :] = xp_ref[CHUNK:CHUNK + SUBLANE, :]

    ri = lax.broadcasted_iota(jnp.int32, (CHUNK, CHUNK), 0)
    ci = lax.broadcasted_iota(jnp.int32, (CHUNK, CHUNK), 1)
    mask = ci <= ri
    tmat = jnp.where(mask, 1.0, 0.0).astype(_BF)
    pos = c * CHUNK + lax.broadcasted_iota(jnp.int32, (CHUNK, LANE), 0)
    dt = jnp.where(pos < l_valid, _softplus(dt_ref[...] + dtb_ref[...]), 0.0)

    hb = h_ref[...].astype(_BF)
    gw = SSM_DIM // SSM_GROUPS

    def hprev_fn(g, cm):
        return _dot_nt(cm, hb[g * gw:(g + 1) * gw, :])

    acum, atot = _ssd_chunk(xa_ref, dt, a_ref[...], mask, tmat, hprev_fn, d_ref, y_ref, xw_ref)
    _ssd_gate_norm(y_ref[...], z_ref[...], ng_ref, o_ref)

    xw_t = xw_ref[...].T
    dec = jnp.exp(atot)
    for g in range(SSM_GROUPS):
        bm = xa_ref[:, SSM_DIM + g * SSM_STATE:SSM_DIM + (g + 1) * SSM_STATE].astype(_BF)
        st = _dot(xw_t[g * gw:(g + 1) * gw, :].astype(_BF), bm)
        for hh in range(SSM_HEADS // SSM_GROUPS):
            h = g * (SSM_HEADS // SSM_GROUPS) + hh
            hs = slice(h * SSM_HEAD_DIM, (h + 1) * SSM_HEAD_DIM)
            h_ref[hs, :] = h_ref[hs, :] * dec[:, h:h + 1] + st[hh * SSM_HEAD_DIM:(hh + 1) * SSM_HEAD_DIM, :]

    @pl.when(c == pl.num_programs(1) - 1)
    def _():
        hout_ref[...] = h_ref[...]


def _ssd_prompt(u, lw, nb, lp, l_valid):
    t = u.shape[0]
    nc = lp // CHUNK
    row = lambda b, i: b * nc + i
    full = lambda shape: pl.BlockSpec(shape, lambda b, i: (0,) * len(shape))
    return pl.pallas_call(
        functools.partial(_ssd_prompt_kernel, l_valid=l_valid),
        grid=(nb, nc),
        in_specs=[
            pl.BlockSpec((CHUNK, XBC_DIM), lambda b, i: (row(b, i), U_XBC // XBC_DIM)),
            pl.BlockSpec((CHUNK, SSM_DIM), lambda b, i: (row(b, i), U_Z // SSM_DIM)),
            pl.BlockSpec((CHUNK, LANE), lambda b, i: (row(b, i), U_DT // LANE)),
            full((SSM_CONV, XBC_DIM)), full((1, XBC_DIM)), full((1, LANE)), full((1, LANE)),
            full((1, SSM_DIM)), full((1, SSM_DIM)),
        ],
        out_specs=[
            pl.BlockSpec((CHUNK, SSM_DIM), lambda b, i: (row(b, i), 0)),
            pl.BlockSpec((None, SSM_DIM, SSM_STATE), lambda b, i: (b, 0, 0)),
        ],
        out_shape=[jax.ShapeDtypeStruct((t, SSM_DIM), _BF), jax.ShapeDtypeStruct((nb, SSM_DIM, SSM_STATE), _F32)],
        scratch_shapes=[
            pltpu.VMEM((CHUNK + SUBLANE, XBC_DIM), _F32), pltpu.VMEM((CHUNK, XBC_DIM), _F32),
            pltpu.VMEM((CHUNK, SSM_DIM), _F32), pltpu.VMEM((CHUNK, SSM_DIM), _F32),
            pltpu.VMEM((SSM_DIM, SSM_STATE), _F32),
        ],
        compiler_params=_cparams(("parallel", "arbitrary")),
        name="ssd_prompt",
    )(u, u, u, lw["scw"], lw["scb"], lw["dtb"], lw["a"], lw["dskip"], lw["sng"])


def _ssd_sample_kernel(xbc_ref, z_ref, dt_ref, left_ref, h0_ref, cw_ref, cb_ref, dtb_ref, a_ref, d_ref, ng_ref,
                       o_ref, hout_ref, xp_ref, xa_ref, y_ref, xw_ref, yo_ref):
    xp_ref[:, 0:SUBLANE, :] = left_ref[...]
    xp_ref[:, SUBLANE:2 * SUBLANE, :] = xbc_ref[...].reshape(REQ_BLK, SUBLANE, XBC_DIM)
    for c in range(XBC_DIM // LANE):
        cs = slice(c * LANE, (c + 1) * LANE)
        acc = jnp.zeros((REQ_BLK, SUBLANE, LANE), _F32) + cb_ref[:, cs][None]
        for k in range(SSM_CONV):
            off = SUBLANE - (SSM_CONV - 1) + k
            acc = acc + cw_ref[k:k + 1, cs][None] * xp_ref[:, off:off + SUBLANE, cs]
        xa_ref[:, cs] = _silu(acc).reshape(CHUNK, LANE)

    ri = lax.broadcasted_iota(jnp.int32, (CHUNK, CHUNK), 0)
    ci = lax.broadcasted_iota(jnp.int32, (CHUNK, CHUNK), 1)
    same = (ri >> 3) == (ci >> 3)
    mask = jnp.logical_and(same, ci <= ri)
    tmat = jnp.where(mask, 1.0, 0.0).astype(_BF)
    lastmat = jnp.where(same, 1.0, 0.0).astype(_BF)
    dt = _softplus(dt_ref[...] + dtb_ref[...])
    gw = SSM_DIM // SSM_GROUPS

    for r in range(REQ_BLK):
        rs = slice(r * SUBLANE, (r + 1) * SUBLANE)
        for g in range(SSM_GROUPS):
            cm = xa_ref[rs, SSM_DIM + (SSM_GROUPS + g) * SSM_STATE:SSM_DIM + (SSM_GROUPS + g + 1) * SSM_STATE]
            hb = h0_ref[r, g * gw:(g + 1) * gw, :].astype(_BF)
            yo_ref[rs, g * gw:(g + 1) * gw] = _dot_nt(cm.astype(_BF), hb)

    def hprev_fn(g, cm):
        return yo_ref[:, g * gw:(g + 1) * gw]

    acum, atot = _ssd_chunk(xa_ref, dt, a_ref[...], mask, tmat, hprev_fn, d_ref, y_ref, xw_ref, lastmat=lastmat)
    _ssd_gate_norm(y_ref[...], z_ref[...], ng_ref, o_ref)

    xw_t = xw_ref[...].T
    dec = jnp.exp(atot)
    lane_req = lax.broadcasted_iota(jnp.int32, (gw, CHUNK), 1) >> 3
    for g in range(SSM_GROUPS):
        bm = xa_ref[:, SSM_DIM + g * SSM_STATE:SSM_DIM + (g + 1) * SSM_STATE].astype(_BF)
        xg = xw_t[g * gw:(g + 1) * gw, :]
        for r in range(REQ_BLK):
            st = _dot(jnp.where(lane_req == r, xg, 0.0).astype(_BF), bm)
            for hh in range(SSM_HEADS // SSM_GROUPS):
                h = g * (SSM_HEADS // SSM_GROUPS) + hh
                hs = slice(h * SSM_HEAD_DIM, (h + 1) * SSM_HEAD_DIM)
                d = dec[r * SUBLANE:r * SUBLANE + 1, h:h + 1]
                hout_ref[r, hs, :] = h0_ref[r, hs, :] * d + st[hh * SSM_HEAD_DIM:(hh + 1) * SSM_HEAD_DIM, :]


def _ssd_sample(u, left8, h0, lw):
    t = u.shape[0]
    nreq = t // SUBLANE
    full = lambda shape: pl.BlockSpec(shape, lambda i: (0,) * len(shape))
    return pl.pallas_call(
        _ssd_sample_kernel,
        grid=(nreq // REQ_BLK,),
        in_specs=[
            pl.BlockSpec((CHUNK, XBC_DIM), lambda i: (i, U_XBC // XBC_DIM)),
            pl.BlockSpec((CHUNK, SSM_DIM), lambda i: (i, U_Z // SSM_DIM)),
            pl.BlockSpec((CHUNK, LANE), lambda i: (i, U_DT // LANE)),
            pl.BlockSpec((REQ_BLK, SUBLANE, XBC_DIM), lambda i: (i, 0, 0)),
            pl.BlockSpec((REQ_BLK, SSM_DIM, SSM_STATE), lambda i: (i, 0, 0)),
            full((SSM_CONV, XBC_DIM)), full((1, XBC_DIM)), full((1, LANE)), full((1, LANE)),
            full((1, SSM_DIM)), full((1, SSM_DIM)),
        ],
        out_specs=[
            pl.BlockSpec((CHUNK, SSM_DIM), lambda i: (i, 0)),
            pl.BlockSpec((REQ_BLK, SSM_DIM, SSM_STATE), lambda i: (i, 0, 0)),
        ],
        out_shape=[jax.ShapeDtypeStruct((t, SSM_DIM), _BF), jax.ShapeDtypeStruct((nreq, SSM_DIM, SSM_STATE), _F32)],
        scratch_shapes=[
            pltpu.VMEM((REQ_BLK, 2 * SUBLANE, XBC_DIM), _F32), pltpu.VMEM((CHUNK, XBC_DIM), _F32),
            pltpu.VMEM((CHUNK, SSM_DIM), _F32), pltpu.VMEM((CHUNK, SSM_DIM), _F32),
            pltpu.VMEM((CHUNK, SSM_DIM), _F32),
        ],
        compiler_params=_cparams(("parallel",)),
        name="ssd_sample",
    )(u, u, u, left8, h0, lw["scw"], lw["scb"], lw["dtb"], lw["a"], lw["dskip"], lw["sng"])


def _outproj_kernel(cy_ref, my_ref, sy_ref, h_ref, w_ref, o_ref, mix_ref):
    @pl.when(pl.program_id(1) == 0)
    def _():
        mix_ref[:, 0:CONV_DIM] = cy_ref[...].astype(_BF)
        mix_ref[:, CONV_DIM:CONV_DIM + MLA_DIM] = my_ref[...].astype(_BF)
        mix_ref[:, CONV_DIM + MLA_DIM:] = sy_ref[...].astype(_BF)

    o_ref[...] = h_ref[...] + _dot(mix_ref[...], w_ref[...])


def _outproj(cy, my, sy, h, w, tm, tn=1024):
    t = h.shape[0]
    return pl.pallas_call(
        _outproj_kernel,
        grid=(t // tm, D_MODEL // tn),
        in_specs=[
            pl.BlockSpec((tm, CONV_DIM), lambda i, j: (i, 0)),
            pl.BlockSpec((tm, MLA_DIM), lambda i, j: (i, 0)),
            pl.BlockSpec((tm, SSM_DIM), lambda i, j: (i, 0)),
            pl.BlockSpec((tm, tn), lambda i, j: (i, j)),
            pl.BlockSpec((D_MODEL, tn), lambda i, j: (0, j)),
        ],
        out_specs=pl.BlockSpec((tm, tn), lambda i, j: (i, j)),
        out_shape=jax.ShapeDtypeStruct((t, D_MODEL), _F32),
        scratch_shapes=[pltpu.VMEM((tm, D_MODEL), _BF)],
        compiler_params=_cparams(("parallel", "arbitrary")),
        name="outproj",
    )(cy, my, sy, h, w)


def _final_norm_kernel(x_ref, g_ref, o_ref):
    x = x_ref[...]
    ms = jnp.mean(x * x, axis=-1, keepdims=True)
    o_ref[...] = x * lax.rsqrt(ms + EPS) * g_ref[...]


def _final_norm(h, g, tm):
    t = h.shape[0]
    return pl.pallas_call(
        _final_norm_kernel,
        grid=(t // tm,),
        in_specs=[pl.BlockSpec((tm, D_MODEL), lambda i: (i, 0)), pl.BlockSpec((1, D_MODEL), lambda i: (0, 0))],
        out_specs=pl.BlockSpec((tm, D_MODEL), lambda i: (i, 0)),
        out_shape=jax.ShapeDtypeStruct((t, D_MODEL), _F32),
        compiler_params=_cparams(("parallel",)),
        name="final_norm",
    )(h, g)


def _rot_half_cols(w):
    half = ROPE_DIM // 2
    return jnp.concatenate([-w[..., half:], w[..., :half]], axis=-1)


def _prep_weights(norm_g, w_in, conv_dw_w, conv_dw_b, conv_ln_g, conv_ln_b, conv_pw_w, conv_pw_b, mla_qa_g,
                  mla_wq_b, mla_kva_g, mla_wkv_b, ssm_conv_w, ssm_conv_b, ssm_dt_bias, ssm_A_log, ssm_D,
                  ssm_norm_g, w_out):
    o = 0
    parts = {}
    for name, width in (("glu", 2 * CONV_DIM), ("cg", CONV_DIM), ("cq", Q_LORA), ("ckv", KV_LORA),
                        ("kpe", ROPE_DIM), ("mg", MLA_DIM), ("z", SSM_DIM), ("xbc", XBC_DIM), ("dt", SSM_HEADS)):
        parts[name] = w_in[..., o:o + width]
        o += width
    dt_pad = jnp.pad(parts["dt"], ((0, 0), (0, 0), (0, LANE - SSM_HEADS)))
    w_in_p = jnp.concatenate(
        [parts["glu"], parts["mg"], parts["xbc"], parts["cg"], parts["cq"], parts["z"], parts["ckv"],
         parts["kpe"], _rot_half_cols(parts["kpe"]), dt_pad], axis=-1).astype(_BF)

    depth = w_in.shape[0]
    wq_nope = mla_wq_b[..., :NOPE_DIM].reshape(depth, Q_LORA, MLA_HEADS * NOPE_DIM)
    wq_rope = mla_wq_b[..., NOPE_DIM:]
    wq_slots = jnp.concatenate([wq_rope, _rot_half_cols(wq_rope)], axis=-1).reshape(depth, Q_LORA, MLA_HEADS * LANE)
    wq = jnp.concatenate([wq_nope, wq_slots], axis=-1).astype(_BF)
    wuk = jnp.transpose(mla_wkv_b[..., :NOPE_DIM], (0, 2, 3, 1)).astype(_BF)
    wuv = jnp.transpose(mla_wkv_b[..., NOPE_DIM:], (0, 2, 1, 3)).astype(_BF)
    wuv_all = mla_wkv_b[..., NOPE_DIM:].reshape(depth, KV_LORA, MLA_DIM).astype(_BF)

    def row(x):
        return x[:, None, :].astype(_F32)

    pad_heads = lambda x: jnp.pad(x, ((0, 0), (0, LANE - SSM_HEADS)))
    layers = []
    for l in range(depth):
        layers.append({
            "norm_g": row(norm_g)[l], "w_in": w_in_p[l],
            "dww": conv_dw_w[l], "dwb": row(conv_dw_b)[l], "lng": row(conv_ln_g)[l], "lnb": row(conv_ln_b)[l],
            "pww": conv_pw_w[l].astype(_BF), "pwb": row(conv_pw_b)[l],
            "qag": row(mla_qa_g)[l], "wq": wq[l], "wuk": wuk[l], "kvg": row(mla_kva_g)[l],
            "wuv": wuv[l], "wuv_all": wuv_all[l],
            "scw": ssm_conv_w[l], "scb": row(ssm_conv_b)[l],
            "dtb": row(pad_heads(ssm_dt_bias))[l], "a": row(pad_heads(-jnp.exp(ssm_A_log.astype(_F32))))[l],
            "dskip": row(jnp.repeat(ssm_D, SSM_HEAD_DIM, axis=-1))[l], "sng": row(ssm_norm_g)[l],
            "w_out": w_out[l].astype(_BF),
        })
    return layers


def _rope_table(pos):
    inv = ROPE_THETA ** (-jnp.arange(0, ROPE_DIM, 2, dtype=_F32) / ROPE_DIM)
    ang = pos.astype(_F32)[:, None] * inv[None, :]
    c, s = jnp.cos(ang), jnp.sin(ang)
    return jnp.concatenate([c, c, s, s], axis=-1)


def _largest_tile(t, cands):
    for c in cands:
        if t % c == 0:
            return c
    raise ValueError(f"no row tile for {t}")


def kernel(x_prompt, x_sample, cache_ckv, cache_kpe, page_table, state_conv, state_ssm_conv, state_ssm, meta_tokens, norm_g, w_in, conv_dw_w, conv_dw_b, conv_ln_g, conv_ln_b, conv_pw_w, conv_pw_b, mla_qa_g, mla_wq_b, mla_kva_g, mla_wkv_b, ssm_conv_w, ssm_conv_b, ssm_dt_bias, ssm_A_log, ssm_D, ssm_norm_g, w_out, final_g):
    nb, seq, _ = x_prompt.shape
    nreq, dseq, _ = x_sample.shape
    depth = w_in.shape[0]
    n_pages = page_table.shape[1]
    page_size = cache_ckv.shape[2]
    assert dseq == SUBLANE and nreq % REQ_BLK == 0 and n_pages % PAGES_PER_STEP == 0 and page_size == LANE
    l_valid = N_META + seq
    lp = -(-l_valid // ATT_BLK) * ATT_BLK
    tp, ts = nb * lp, nreq * dseq

    layers = _prep_weights(norm_g, w_in, conv_dw_w, conv_dw_b, conv_ln_g, conv_ln_b, conv_pw_w, conv_pw_b,
                           mla_qa_g, mla_wq_b, mla_kva_g, mla_wkv_b, ssm_conv_w, ssm_conv_b, ssm_dt_bias,
                           ssm_A_log, ssm_D, ssm_norm_g, w_out)

    meta = jnp.broadcast_to(meta_tokens[None].astype(_F32), (nb, N_META, D_MODEL))
    hp = jnp.concatenate([meta, x_prompt, jnp.zeros((nb, lp - l_valid, D_MODEL), _F32)], axis=1).reshape(tp, D_MODEL)
    hs = x_sample.reshape(ts, D_MODEL)
    cs_p = jnp.tile(_rope_table(jnp.arange(lp, dtype=jnp.int32)), (nb, 1))
    past_len = n_pages * page_size
    cs_s = jnp.tile(_rope_table(past_len + jnp.arange(dseq, dtype=jnp.int32)), (nreq, 1))
    pt_flat = page_table.reshape(-1).astype(jnp.int32)
    left_conv = jnp.pad(state_conv, ((0, 0), (0, 0), (2, 0), (0, 0)))
    left_ssm = jnp.pad(state_ssm_conv, ((0, 0), (0, 0), (SUBLANE - (SSM_CONV - 1), 0), (0, 0)))
    h0_all = state_ssm.reshape(depth, nreq, SSM_DIM, SSM_STATE)

    tm_p = _largest_tile(tp, (768, 512, 384, 256, 128))
    tm_s = _largest_tile(ts, (512, 256, 128))
    tq_p = _largest_tile(tp, (384, 256, 128))
    tq_s = _largest_tile(ts, (256, 128))

    st_p = [[] for _ in range(5)]
    st_s = [[] for _ in range(5)]
    for l in range(depth):
        lw = layers[l]
        u = _inproj(hp, lw["norm_g"], lw["w_in"], tm_p)
        v, cy = _conv_prompt(u, lw, nb, lp)
        q, kcat, ckv, kpe = _qk(u, cs_p, lw, tq_p, _BF)
        my = _attn_prompt(q, kcat, u, lw["wuv"], nb, lp)
        sy, hfin = _ssd_prompt(u, lw, nb, lp, l_valid)
        hp = _outproj(cy, my, sy, hp, lw["w_out"], tm_p)
        u3 = u.reshape(nb, lp, U_TOTAL)
        st_p[0].append(ckv.reshape(nb, lp, KV_LORA)[:, :l_valid])
        st_p[1].append(kpe.reshape(nb, lp, ROPE_DIM)[:, :l_valid])
        st_p[2].append(v.reshape(nb, lp, CONV_DIM)[:, l_valid - (CONV_WIDTH - 1):l_valid])
        st_p[3].append(u3[:, l_valid - (SSM_CONV - 1):l_valid, U_XBC:U_XBC + XBC_DIM])
        st_p[4].append(hfin.reshape(nb, SSM_HEADS, SSM_HEAD_DIM, SSM_STATE))
        u = _inproj(hs, lw["norm_g"], lw["w_in"], tm_s)
        cst, cy = _conv_sample(u, left_conv[l], lw)
        q, kcat, ckv, kpe = _qk(u, cs_s, lw, tq_s, _F32)
        my = _attn_sample(q, kcat, u, lw["wuv_all"], cache_ckv, cache_kpe, pt_flat, l, n_pages)
        sy, hnew = _ssd_sample(u, left_ssm[l], h0_all[l], lw)
        hs = _outproj(cy, my, sy, hs, lw["w_out"], tm_s)
        u3 = u.reshape(nreq, dseq, U_TOTAL)
        st_s[0].append(ckv.reshape(nreq, dseq, KV_LORA))
        st_s[1].append(kpe.reshape(nreq, dseq, ROPE_DIM))
        st_s[2].append(cst[:, 2:, :])
        st_s[3].append(u3[:, dseq - (SSM_CONV - 1):, U_XBC:U_XBC + XBC_DIM])
        st_s[4].append(hnew.reshape(nreq, SSM_HEADS, SSM_HEAD_DIM, SSM_STATE))

    y_prompt = _final_norm(hp, final_g[None, :].astype(_F32), tm_p).reshape(nb, lp, D_MODEL)[:, N_META:l_valid]
    y_sample = _final_norm(hs, final_g[None, :].astype(_F32), tm_s).reshape(nreq, dseq, D_MODEL)
    outs_p = [jnp.stack(s) for s in st_p]
    outs_s = [jnp.stack(s) for s in st_s]
    return (y_prompt, y_sample, *outs_p, *outs_s)
```

```python
import functools
import math

import jax
import jax.numpy as jnp
from jax import lax
from jax.experimental import pallas as pl
from jax.experimental.pallas import tpu as pltpu

D_MODEL = 2048
DEPTH = 4
N_META = 16
CONV_DIM = 512
CONV_WIDTH = 31
MLA_HEADS = 8
MLA_V_DIM = 128
MLA_DIM = MLA_HEADS * MLA_V_DIM
Q_LORA = 512
KV_LORA = 256
NOPE_DIM = 128
ROPE_DIM = 64
ROPE_THETA = 10000.0
MLA_SCALE = 1.0 / math.sqrt(NOPE_DIM + ROPE_DIM)
SSM_DIM = 512
SSM_HEAD_DIM = 64
SSM_HEADS = 8
SSM_GROUPS = 2
SSM_STATE = 128
SSM_CONV = 4
XBC_DIM = SSM_DIM + 2 * SSM_GROUPS * SSM_STATE
EPS = 1e-6

LANE = 128
SUBLANE = 8
VMEM_LIMIT = 48 * 1024 * 1024

U_GLU = 0
U_MG = 1024
U_XBC = 2048
U_CG = 3072
U_CQ = 3584
U_Z = 4096
U_CKV = 4608
U_KPE = 4864
U_DT = 4992
U_TOTAL = 5120

QK_PAD = 384
CHUNK = 128
ATT_BLK = 256

_BF = jnp.bfloat16
_F32 = jnp.float32


def _cparams(sem):
    return pltpu.CompilerParams(dimension_semantics=sem, vmem_limit_bytes=VMEM_LIMIT)


def _dot(a, b):
    return jnp.dot(a, b, preferred_element_type=_F32)


def _dot_nt(a, b):
    return lax.dot_general(a, b, (((1,), (1,)), ((), ())), preferred_element_type=_F32)


def _sigmoid(x):
    return 1.0 / (1.0 + jnp.exp(-x))


def _silu(x):
    return x * _sigmoid(x)


def _softplus(x):
    return jnp.maximum(x, 0.0) + jnp.log(1.0 + jnp.exp(-jnp.abs(x)))


def _exact_dot_01(mat01_bf16, x):
    hi = x.astype(_BF)
    r1 = x - hi.astype(_F32)
    mid = r1.astype(_BF)
    lo = (r1 - mid.astype(_F32)).astype(_BF)
    return _dot(mat01_bf16, hi) + _dot(mat01_bf16, mid) + _dot(mat01_bf16, lo)


def _inproj_kernel(x_ref, g_ref, w_ref, o_ref, xn_ref):
    @pl.when(pl.program_id(1) == 0)
    def _():
        x = x_ref[...]
        ms = jnp.mean(x * x, axis=-1, keepdims=True)
        xn_ref[...] = (x * lax.rsqrt(ms + EPS) * g_ref[...]).astype(_BF)

    o_ref[...] = _dot(xn_ref[...], w_ref[...])


def _inproj(h, g, w_all, layer, tm, tn=1024):
    t = h.shape[0]
    n = w_all.shape[2]
    return pl.pallas_call(
        _inproj_kernel,
        grid=(t // tm, n // tn),
        in_specs=[
            pl.BlockSpec((tm, D_MODEL), lambda i, j: (i, 0)),
            pl.BlockSpec((1, D_MODEL), lambda i, j: (0, 0)),
            pl.BlockSpec((None, D_MODEL, tn), lambda i, j: (layer, 0, j)),
        ],
        out_specs=pl.BlockSpec((tm, tn), lambda i, j: (i, j)),
        out_shape=jax.ShapeDtypeStruct((t, n), _F32),
        scratch_shapes=[pltpu.VMEM((tm, D_MODEL), _BF)],
        compiler_params=_cparams(("parallel", "arbitrary")),
        name="inproj",
    )(h, g, w_all)


def _conv_tail(y, cg, lng_ref, lnb_ref, pww_ref, pwb_ref):
    mu = jnp.mean(y, axis=-1, keepdims=True)
    yc = y - mu
    var = jnp.mean(yc * yc, axis=-1, keepdims=True)
    yn = yc * lax.rsqrt(var + EPS) * lng_ref[...] + lnb_ref[...]
    s = _silu(yn)
    o = _dot(s.astype(_BF), pww_ref[...]) + pwb_ref[...]
    return o * _silu(cg)


def _conv_prompt_kernel(glu_ref, cg_ref, dww_ref, dwb_ref, lng_ref, lnb_ref, pww_ref, pwb_ref,
                        v_ref, cy_ref, vp_ref, y_ref):
    @pl.when(pl.program_id(1) == 0)
    def _():
        vp_ref[0:32, :] = jnp.zeros((32, CONV_DIM), _F32)

    v = glu_ref[:, :CONV_DIM] * _sigmoid(glu_ref[:, CONV_DIM:])
    v_ref[...] = v
    vp_ref[32:32 + CHUNK, :] = v
    for c in range(CONV_DIM // LANE):
        cs = slice(c * LANE, (c + 1) * LANE)
        acc = jnp.zeros((CHUNK, LANE), _F32)
        for k in range(CONV_WIDTH):
            acc = acc + dww_ref[k:k + 1, cs] * vp_ref[2 + k:2 + k + CHUNK, cs]
        y_ref[:, cs] = acc + dwb_ref[:, cs]
    cy = _conv_tail(y_ref[...], cg_ref[...], lng_ref, lnb_ref, pww_ref, pwb_ref)
    cy_ref[...] = cy.astype(cy_ref.dtype)
    vp_ref[0:32, :] = vp_ref[CHUNK:CHUNK + 32, :]


def _conv_prompt(u, lw, nb, lp):
    t = u.shape[0]
    nt = lp // CHUNK
    row = lambda b, i: b * nt + i
    full = lambda shape: pl.BlockSpec(shape, lambda b, i: (0,) * len(shape))
    return pl.pallas_call(
        _conv_prompt_kernel,
        grid=(nb, nt),
        in_specs=[
            pl.BlockSpec((CHUNK, 2 * CONV_DIM), lambda b, i: (row(b, i), U_GLU // (2 * CONV_DIM))),
            pl.BlockSpec((CHUNK, CONV_DIM), lambda b, i: (row(b, i), U_CG // CONV_DIM)),
            full((CONV_WIDTH, CONV_DIM)), full((1, CONV_DIM)), full((1, CONV_DIM)), full((1, CONV_DIM)),
            full((CONV_DIM, CONV_DIM)), full((1, CONV_DIM)),
        ],
        out_specs=[
            pl.BlockSpec((CHUNK, CONV_DIM), lambda b, i: (row(b, i), 0)),
            pl.BlockSpec((CHUNK, CONV_DIM), lambda b, i: (row(b, i), 0)),
        ],
        out_shape=[jax.ShapeDtypeStruct((t, CONV_DIM), _F32), jax.ShapeDtypeStruct((t, CONV_DIM), _BF)],
        scratch_shapes=[pltpu.VMEM((CHUNK + 32, CONV_DIM), _F32), pltpu.VMEM((CHUNK, CONV_DIM), _F32)],
        compiler_params=_cparams(("parallel", "arbitrary")),
        name="conv_prompt",
    )(u, u, lw["dww"], lw["dwb"], lw["lng"], lw["lnb"], lw["pww"], lw["pwb"])


REQ_BLK = CHUNK // SUBLANE


def _conv_sample_kernel(glu_ref, cg_ref, left_ref, dww_ref, dwb_ref, lng_ref, lnb_ref, pww_ref, pwb_ref,
                        st_ref, cy_ref, vp_ref, y_ref):
    v = glu_ref[:, :CONV_DIM] * _sigmoid(glu_ref[:, CONV_DIM:])
    vp_ref[:, 0:32, :] = left_ref[...]
    vp_ref[:, 32:40, :] = v.reshape(REQ_BLK, SUBLANE, CONV_DIM)
    for c in range(CONV_DIM // LANE):
        cs = slice(c * LANE, (c + 1) * LANE)
        acc = jnp.zeros((REQ_BLK, SUBLANE, LANE), _F32)
        for k in range(CONV_WIDTH):
            acc = acc + dww_ref[k:k + 1, cs][None] * vp_ref[:, 2 + k:2 + k + SUBLANE, cs]
        y_ref[:, :, cs] = acc + dwb_ref[:, cs][None]
    y = y_ref[...].reshape(CHUNK, CONV_DIM)
    cy = _conv_tail(y, cg_ref[...], lng_ref, lnb_ref, pww_ref, pwb_ref)
    cy_ref[...] = cy.astype(cy_ref.dtype)
    st_ref[...] = vp_ref[:, 8:40, :]


def _conv_sample(u, left32, lw):
    t = u.shape[0]
    nreq = t // SUBLANE
    full = lambda shape: pl.BlockSpec(shape, lambda i: (0,) * len(shape))
    return pl.pallas_call(
        _conv_sample_kernel,
        grid=(nreq // REQ_BLK,),
        in_specs=[
            pl.BlockSpec((CHUNK, 2 * CONV_DIM), lambda i: (i, U_GLU // (2 * CONV_DIM))),
            pl.BlockSpec((CHUNK, CONV_DIM), lambda i: (i, U_CG // CONV_DIM)),
            pl.BlockSpec((REQ_BLK, 32, CONV_DIM), lambda i: (i, 0, 0)),
            full((CONV_WIDTH, CONV_DIM)), full((1, CONV_DIM)), full((1, CONV_DIM)), full((1, CONV_DIM)),
            full((CONV_DIM, CONV_DIM)), full((1, CONV_DIM)),
        ],
        out_specs=[
            pl.BlockSpec((REQ_BLK, 32, CONV_DIM), lambda i: (i, 0, 0)),
            pl.BlockSpec((CHUNK, CONV_DIM), lambda i: (i, 0)),
        ],
        out_shape=[jax.ShapeDtypeStruct((nreq, 32, CONV_DIM), _F32), jax.ShapeDtypeStruct((t, CONV_DIM), _BF)],
        scratch_shapes=[pltpu.VMEM((REQ_BLK, 40, CONV_DIM), _F32), pltpu.VMEM((REQ_BLK, SUBLANE, CONV_DIM), _F32)],
        compiler_params=_cparams(("parallel",)),
        name="conv_sample",
    )(u, u, left32, lw["dww"], lw["dwb"], lw["lng"], lw["lnb"], lw["pww"], lw["pwb"])


def _rope_slot(slot, cs):
    r = slot * cs
    r = r + pltpu.roll(r, 64, axis=1)
    lane = lax.broadcasted_iota(jnp.int32, r.shape, 1)
    return jnp.where(lane < ROPE_DIM, r, 0.0)


def _qk_kernel(cq_ref, ckv_ref, kpe_ref, cs_ref, qag_ref, wq_ref, wuk_ref, kvg_ref,
               q_ref, kcat_ref, ckvo_ref, kpeo_ref):
    cq = cq_ref[...]
    ms = jnp.mean(cq * cq, axis=-1, keepdims=True)
    cqn = (cq * lax.rsqrt(ms + EPS) * qag_ref[...]).astype(_BF)
    q = _dot(cqn, wq_ref[...])
    cs = cs_ref[...]
    for h in range(MLA_HEADS):
        qn = q[:, h * NOPE_DIM:(h + 1) * NOPE_DIM].astype(_BF)
        ql = _dot(qn, wuk_ref[h])
        q_ref[h, :, 0:KV_LORA] = ql.astype(q_ref.dtype)
        slot = q[:, MLA_HEADS * NOPE_DIM + h * LANE:MLA_HEADS * NOPE_DIM + (h + 1) * LANE]
        q_ref[h, :, KV_LORA:QK_PAD] = _rope_slot(slot, cs).astype(q_ref.dtype)
    c = ckv_ref[...]
    ms = jnp.mean(c * c, axis=-1, keepdims=True)
    ckv = c * lax.rsqrt(ms + EPS) * kvg_ref[...]
    ckvo_ref[...] = ckv
    kcat_ref[:, 0:KV_LORA] = ckv.astype(kcat_ref.dtype)
    kr = _rope_slot(kpe_ref[...], cs)
    kcat_ref[:, KV_LORA:QK_PAD] = kr.astype(kcat_ref.dtype)
    kpeo_ref[...] = kr[:, :ROPE_DIM]


def _qk(u, cs, lw, tm, qdtype):
    t = u.shape[0]
    full = lambda shape: pl.BlockSpec(shape, lambda i: (0,) * len(shape))
    return pl.pallas_call(
        _qk_kernel,
        grid=(t // tm,),
        in_specs=[
            pl.BlockSpec((tm, Q_LORA), lambda i: (i, U_CQ // Q_LORA)),
            pl.BlockSpec((tm, KV_LORA), lambda i: (i, U_CKV // KV_LORA)),
            pl.BlockSpec((tm, LANE), lambda i: (i, U_KPE // LANE)),
            pl.BlockSpec((tm, LANE), lambda i: (i, 0)),
            full((1, Q_LORA)), full((Q_LORA, 2 * MLA_HEADS * NOPE_DIM)),
            full((MLA_HEADS, NOPE_DIM, KV_LORA)), full((1, KV_LORA)),
        ],
        out_specs=[
            pl.BlockSpec((MLA_HEADS, tm, QK_PAD), lambda i: (0, i, 0)),
            pl.BlockSpec((tm, QK_PAD), lambda i: (i, 0)),
            pl.BlockSpec((tm, KV_LORA), lambda i: (i, 0)),
            pl.BlockSpec((tm, ROPE_DIM), lambda i: (i, 0)),
        ],
        out_shape=[
            jax.ShapeDtypeStruct((MLA_HEADS, t, QK_PAD), qdtype),
            jax.ShapeDtypeStruct((t, QK_PAD), qdtype),
            jax.ShapeDtypeStruct((t, KV_LORA), _F32),
            jax.ShapeDtypeStruct((t, ROPE_DIM), _F32),
        ],
        compiler_params=_cparams(("parallel",)),
        name="qk",
    )(u, u, u, cs, lw["qag"], lw["wq"], lw["wuk"], lw["kvg"])


def _qkt_kernel(cq_ref, ckv_ref, kpe_ref, cs_ref, cst_ref, qag_ref, wqt_ref, wukt_ref, kvg_ref,
                qt_ref, kcat_ref, vt_ref, ckvo_ref, kpeo_ref):
    tm = ATT_BLK
    cq = cq_ref[...]
    ms = jnp.mean(cq * cq, axis=-1, keepdims=True)
    cqn_t = (cq * lax.rsqrt(ms + EPS) * qag_ref[...]).T.astype(_BF)
    qt = _dot(wqt_ref[...], cqn_t)
    cst = cst_ref[...]
    for h in range(MLA_HEADS):
        cols = slice(h * tm, (h + 1) * tm)
        qn_t = qt[h * NOPE_DIM:(h + 1) * NOPE_DIM].astype(_BF)
        qt_ref[0:KV_LORA, cols] = _dot(wukt_ref[h], qn_t).astype(qt_ref.dtype)
        r = qt[MLA_HEADS * NOPE_DIM + h * LANE:MLA_HEADS * NOPE_DIM + (h + 1) * LANE] * cst
        qt_ref[KV_LORA:KV_LORA + ROPE_DIM, cols] = (r[:ROPE_DIM] + r[ROPE_DIM:]).astype(qt_ref.dtype)
        qt_ref[KV_LORA + ROPE_DIM:QK_PAD, cols] = jnp.zeros((QK_PAD - KV_LORA - ROPE_DIM, tm), qt_ref.dtype)
    c = ckv_ref[...]
    ms = jnp.mean(c * c, axis=-1, keepdims=True)
    ckv = c * lax.rsqrt(ms + EPS) * kvg_ref[...]
    ckvo_ref[...] = ckv
    kcat_ref[:, 0:KV_LORA] = ckv.astype(kcat_ref.dtype)
    vt_ref[...] = ckv.T.astype(vt_ref.dtype)
    kr = _rope_slot(kpe_ref[...], cs_ref[...])
    kcat_ref[:, KV_LORA:QK_PAD] = kr.astype(kcat_ref.dtype)
    kpeo_ref[...] = kr[:, :ROPE_DIM]


def _qkt(u, cs, cst, lw, nb, lp):
    t = u.shape[0]
    tm = ATT_BLK
    nblk = t // tm
    full = lambda shape: pl.BlockSpec(shape, lambda i: (0,) * len(shape))
    return pl.pallas_call(
        _qkt_kernel,
        grid=(nblk,),
        in_specs=[
            pl.BlockSpec((tm, Q_LORA), lambda i: (i, U_CQ // Q_LORA)),
            pl.BlockSpec((tm, KV_LORA), lambda i: (i, U_CKV // KV_LORA)),
            pl.BlockSpec((tm, LANE), lambda i: (i, U_KPE // LANE)),
            pl.BlockSpec((tm, LANE), lambda i: (i, 0)),
            pl.BlockSpec((LANE, tm), lambda i: (0, i)),
            full((1, Q_LORA)), full((2 * MLA_HEADS * NOPE_DIM, Q_LORA)),
            full((MLA_HEADS, KV_LORA, NOPE_DIM)), full((1, KV_LORA)),
        ],
        out_specs=[
            pl.BlockSpec((None, QK_PAD, MLA_HEADS * tm), lambda i: (i, 0, 0)),
            pl.BlockSpec((tm, QK_PAD), lambda i: (i, 0)),
            pl.BlockSpec((None, KV_LORA, tm), lambda i: (i, 0, 0)),
            pl.BlockSpec((tm, KV_LORA), lambda i: (i, 0)),
            pl.BlockSpec((tm, ROPE_DIM), lambda i: (i, 0)),
        ],
        out_shape=[
            jax.ShapeDtypeStruct((nblk, QK_PAD, MLA_HEADS * tm), _BF),
            jax.ShapeDtypeStruct((t, QK_PAD), _BF),
            jax.ShapeDtypeStruct((nblk, KV_LORA, tm), _BF),
            jax.ShapeDtypeStruct((t, KV_LORA), _F32),
            jax.ShapeDtypeStruct((t, ROPE_DIM), _F32),
        ],
        compiler_params=_cparams(("parallel",)),
        name="qkt",
    )(u, u, u, cs, cst, lw["qag"], lw["wqt"], lw["wukt"], lw["kvg"])


def _attn_prompt_kernel(qt_ref, k_ref, vt_ref, mg_ref, wuvt_ref, o_ref, m_ref, l_ref, acc_ref):
    qi = pl.program_id(1)
    ncol = MLA_HEADS * ATT_BLK
    kk = lax.broadcasted_iota(jnp.int32, (ATT_BLK, ATT_BLK), 0)
    qq = lax.broadcasted_iota(jnp.int32, (ATT_BLK, ATT_BLK), 1)
    diag_mask = jnp.concatenate([kk <= qq] * MLA_HEADS, axis=1)

    m_ref[...] = jnp.full((1, ncol), -jnp.inf, _F32)
    l_ref[...] = jnp.zeros((1, ncol), _F32)
    acc_ref[...] = jnp.zeros((KV_LORA, ncol), _F32)

    def chunk(c, masked):
        kc = k_ref[pl.ds(pl.multiple_of(c * ATT_BLK, ATT_BLK), ATT_BLK), :]
        s = _dot(kc, qt_ref[...]) * MLA_SCALE
        if masked:
            s = jnp.where(diag_mask, s, -jnp.inf)
        m_prev = m_ref[...]
        m_new = jnp.maximum(m_prev, jnp.max(s, axis=0, keepdims=True))
        alpha = jnp.exp(m_prev - m_new)
        p = jnp.exp(s - m_new)
        l_ref[...] = alpha * l_ref[...] + jnp.sum(p, axis=0, keepdims=True)
        acc_ref[...] = alpha * acc_ref[...] + _dot(vt_ref[c], p.astype(_BF))
        m_ref[...] = m_new

    def body(c, carry):
        chunk(c, False)
        return carry

    lax.fori_loop(0, qi, body, 0)
    chunk(qi, True)
    for h in range(MLA_HEADS):
        cols = slice(h * ATT_BLK, (h + 1) * ATT_BLK)
        o_t = (acc_ref[:, cols] / l_ref[:, cols]).astype(_BF)
        y = _dot(wuvt_ref[h], o_t).T
        g = mg_ref[:, h * MLA_V_DIM:(h + 1) * MLA_V_DIM]
        o_ref[:, h * MLA_V_DIM:(h + 1) * MLA_V_DIM] = (y * _silu(g)).astype(o_ref.dtype)


def _attn_prompt(qt, kcat, vt, u, wuvt, nb, lp):
    t = kcat.shape[0]
    nq = lp // ATT_BLK
    return pl.pallas_call(
        _attn_prompt_kernel,
        grid=(nb, nq),
        in_specs=[
            pl.BlockSpec((None, QK_PAD, MLA_HEADS * ATT_BLK), lambda b, i: (b * nq + i, 0, 0)),
            pl.BlockSpec((None, lp, QK_PAD), lambda b, i: (b, 0, 0)),
            pl.BlockSpec((None, nq, KV_LORA, ATT_BLK), lambda b, i: (b, 0, 0, 0)),
            pl.BlockSpec((ATT_BLK, MLA_DIM), lambda b, i: (b * nq + i, U_MG // MLA_DIM)),
            pl.BlockSpec((MLA_HEADS, MLA_V_DIM, KV_LORA), lambda b, i: (0, 0, 0)),
        ],
        out_specs=pl.BlockSpec((ATT_BLK, MLA_DIM), lambda b, i: (b * nq + i, 0)),
        out_shape=jax.ShapeDtypeStruct((t, MLA_DIM), _BF),
        scratch_shapes=[
            pltpu.VMEM((1, MLA_HEADS * ATT_BLK), _F32),
            pltpu.VMEM((1, MLA_HEADS * ATT_BLK), _F32),
            pltpu.VMEM((KV_LORA, MLA_HEADS * ATT_BLK), _F32),
        ],
        compiler_params=_cparams(("parallel", "arbitrary")),
        name="attn_prompt",
    )(qt, kcat.reshape(nb, lp, QK_PAD), vt.reshape(nb, nq, KV_LORA, ATT_BLK), u, wuvt)


def _page_copies(pt_ref, ckv_hbm, kpe_hbm, ckv_buf, kpe_buf, sem, req, slot, layer, n_pages):
    copies = []
    for i in range(n_pages):
        pg = pt_ref[req * n_pages + i]
        copies.append(pltpu.make_async_copy(
            ckv_hbm.at[layer, pg], ckv_buf.at[slot, pl.ds(i * LANE, LANE), :], sem.at[0, slot]))
        copies.append(pltpu.make_async_copy(
            kpe_hbm.at[layer, pg], kpe_buf.at[slot, :, pl.ds(i * LANE, LANE)], sem.at[1, slot]))
    return copies


def _attn_sample_kernel(pt_ref, q_ref, kn_ref, mg_ref, wuv_ref, ckv_hbm, kpe_hbm, o_ref,
                        ckv_buf, kpe_buf, kbf_ref, sem, *, layer, n_pages):
    b = pl.program_id(0)
    slot = b & 1
    rows = MLA_HEADS * SUBLANE
    fetch = functools.partial(_page_copies, pt_ref, ckv_hbm, kpe_hbm, ckv_buf, kpe_buf, sem,
                              layer=layer, n_pages=n_pages)

    @pl.when(b == 0)
    def _():
        for c in fetch(req=b, slot=slot):
            c.start()

    @pl.when(b + 1 < pl.num_programs(0))
    def _():
        for c in fetch(req=b + 1, slot=1 - slot):
            c.start()

    for c in fetch(req=b, slot=slot):
        c.wait()

    q = q_ref[...].reshape(rows, QK_PAD)
    qb = q.astype(_BF)
    kbf_ref[...] = ckv_buf[slot].astype(_BF)
    kpt = kpe_buf[slot].astype(_BF)
    s = (_dot_nt(qb[:, :KV_LORA], kbf_ref[...]) + _dot(qb[:, KV_LORA:KV_LORA + ROPE_DIM], kpt)) * MLA_SCALE
    kn = jnp.concatenate([kn_ref[...], jnp.zeros((LANE - SUBLANE, QK_PAD), _F32)], axis=0).astype(_BF)
    sn = _dot_nt(qb, kn) * MLA_SCALE
    tq = lax.broadcasted_iota(jnp.int32, sn.shape, 0) & (SUBLANE - 1)
    tk = lax.broadcasted_iota(jnp.int32, sn.shape, 1)
    sn = jnp.where(tk <= tq, sn, -jnp.inf)
    m = jnp.maximum(jnp.max(s, axis=-1, keepdims=True), jnp.max(sn, axis=-1, keepdims=True))
    p = jnp.exp(s - m)
    pn = jnp.exp(sn - m)
    l = jnp.sum(p, axis=-1, keepdims=True) + jnp.sum(pn, axis=-1, keepdims=True)
    acc = _dot(p.astype(_BF), kbf_ref[...]) + _dot(pn.astype(_BF), kn[:, :KV_LORA])
    o = (acc / l).astype(_BF)
    y = _dot(o, wuv_ref[...])
    mg = mg_ref[...]
    for h in range(MLA_HEADS):
        cs = slice(h * MLA_V_DIM, (h + 1) * MLA_V_DIM)
        yh = y[h * SUBLANE:(h + 1) * SUBLANE, cs]
        o_ref[:, cs] = (yh * _silu(mg[:, cs])).astype(o_ref.dtype)


def _attn_sample(q, kcat, u, wuv_all, cache_ckv, cache_kpe_t, pt_flat, layer, n_pages):
    t = kcat.shape[0]
    nreq = t // SUBLANE
    nkeys = n_pages * LANE
    grid_spec = pltpu.PrefetchScalarGridSpec(
        num_scalar_prefetch=1,
        grid=(nreq,),
        in_specs=[
            pl.BlockSpec((MLA_HEADS, SUBLANE, QK_PAD), lambda b, pt: (0, b, 0)),
            pl.BlockSpec((SUBLANE, QK_PAD), lambda b, pt: (b, 0)),
            pl.BlockSpec((SUBLANE, MLA_DIM), lambda b, pt: (b, U_MG // MLA_DIM)),
            pl.BlockSpec((KV_LORA, MLA_DIM), lambda b, pt: (0, 0)),
            pl.BlockSpec(memory_space=pl.ANY),
            pl.BlockSpec(memory_space=pl.ANY),
        ],
        out_specs=pl.BlockSpec((SUBLANE, MLA_DIM), lambda b, pt: (b, 0)),
        scratch_shapes=[
            pltpu.VMEM((2, nkeys, KV_LORA), _F32),
            pltpu.VMEM((2, ROPE_DIM, nkeys), _F32),
            pltpu.VMEM((nkeys, KV_LORA), _BF),
            pltpu.SemaphoreType.DMA((2, 2)),
        ],
    )
    return pl.pallas_call(
        functools.partial(_attn_sample_kernel, layer=layer, n_pages=n_pages),
        grid_spec=grid_spec,
        out_shape=jax.ShapeDtypeStruct((t, MLA_DIM), _F32),
        compiler_params=_cparams(("arbitrary",)),
        name="attn_sample",
    )(pt_flat, q, kcat, u, wuv_all, cache_ckv, cache_kpe_t)


def _ssd_chunk(xa_ref, dt, a_row, mask, tmat, hprev_fn, d_ref, y_ref, xw_ref, lastmat=None):
    da = dt * a_row
    acum = _exact_dot_01(tmat, da)
    atot = acum[CHUNK - 1:CHUNK, :] if lastmat is None else _exact_dot_01(lastmat, da)
    acum_t = acum.T
    dt_t = dt.T
    ea = jnp.exp(acum)
    w = jnp.exp(atot - acum) * dt
    heads_per_group = SSM_HEADS // SSM_GROUPS
    for g in range(SSM_GROUPS):
        bm = xa_ref[:, SSM_DIM + g * SSM_STATE:SSM_DIM + (g + 1) * SSM_STATE].astype(_BF)
        cm = xa_ref[:, SSM_DIM + (SSM_GROUPS + g) * SSM_STATE:SSM_DIM + (SSM_GROUPS + g + 1) * SSM_STATE].astype(_BF)
        gmat = _dot_nt(cm, bm)
        yo = hprev_fn(g, cm)
        for hh in range(heads_per_group):
            h = g * heads_per_group + hh
            hs = slice(h * SSM_HEAD_DIM, (h + 1) * SSM_HEAD_DIM)
            seg = acum[:, h:h + 1] - acum_t[h:h + 1, :]
            lmat = jnp.exp(jnp.where(mask, seg, -jnp.inf))
            sc = gmat * (lmat * dt_t[h:h + 1, :])
            xh = xa_ref[:, hs]
            yd = _dot(sc.astype(_BF), xh.astype(_BF))
            yoh = yo[:, hh * SSM_HEAD_DIM:(hh + 1) * SSM_HEAD_DIM] * ea[:, h:h + 1]
            y_ref[:, hs] = yd + yoh + xh * d_ref[:, hs]
            xw_ref[:, hs] = xh * w[:, h:h + 1]
    return acum, atot


def _ssd_gate_norm(y, z, ng_ref, o_ref):
    gated = y * _silu(z)
    gw = SSM_DIM // SSM_GROUPS
    for g in range(SSM_GROUPS):
        cs = slice(g * gw, (g + 1) * gw)
        gg = gated[:, cs]
        ms = jnp.mean(gg * gg, axis=-1, keepdims=True)
        o_ref[:, cs] = (gg * lax.rsqrt(ms + EPS) * ng_ref[:, cs]).astype(o_ref.dtype)


def _short_conv_silu(xp_ref, cw_ref, cb_ref, xa_ref, lead):
    for c in range(XBC_DIM // LANE):
        cs = slice(c * LANE, (c + 1) * LANE)
        acc = jnp.zeros((CHUNK, LANE), _F32) + cb_ref[:, cs]
        for k in range(SSM_CONV):
            off = lead - (SSM_CONV - 1) + k
            acc = acc + cw_ref[k:k + 1, cs] * xp_ref[off:off + CHUNK, cs]
        xa_ref[:, cs] = _silu(acc)


def _ssd_prompt_kernel(xbc_ref, z_ref, dt_ref, cw_ref, cb_ref, dtb_ref, a_ref, d_ref, ng_ref,
                       o_ref, hout_ref, xp_ref, xa_ref, y_ref, xw_ref, h_ref, *, l_valid):
    c = pl.program_id(1)

    @pl.when(c == 0)
    def _():
        xp_ref[0:SUBLANE, :] = jnp.zeros((SUBLANE, XBC_DIM), _F32)
        h_ref[...] = jnp.zeros(h_ref.shape, _F32)

    xp_ref[SUBLANE:SUBLANE + CHUNK, :] = xbc_ref[...]
    _short_conv_silu(xp_ref, cw_ref, cb_ref, xa_ref, SUBLANE)
    xp_ref[0:SUBLANE, :] = xp_ref[CHUNK:CHUNK + SUBLANE, :]

    ri = lax.broadcasted_iota(jnp.int32, (CHUNK, CHUNK), 0)
    ci = lax.broadcasted_iota(jnp.int32, (CHUNK, CHUNK), 1)
    mask = ci <= ri
    tmat = jnp.where(mask, 1.0, 0.0).astype(_BF)
    pos = c * CHUNK + lax.broadcasted_iota(jnp.int32, (CHUNK, LANE), 0)
    dt = jnp.where(pos < l_valid, _softplus(dt_ref[...] + dtb_ref[...]), 0.0)

    hb = h_ref[...].astype(_BF)
    gw = SSM_DIM // SSM_GROUPS

    def hprev_fn(g, cm):
        return _dot_nt(cm, hb[g * gw:(g + 1) * gw, :])

    acum, atot = _ssd_chunk(xa_ref, dt, a_ref[...], mask, tmat, hprev_fn, d_ref, y_ref, xw_ref)
    _ssd_gate_norm(y_ref[...], z_ref[...], ng_ref, o_ref)

    xw_t = xw_ref[...].T
    dec = jnp.exp(atot)
    for g in range(SSM_GROUPS):
        bm = xa_ref[:, SSM_DIM + g * SSM_STATE:SSM_DIM + (g + 1) * SSM_STATE].astype(_BF)
        st = _dot(xw_t[g * gw:(g + 1) * gw, :].astype(_BF), bm)
        for hh in range(SSM_HEADS // SSM_GROUPS):
            h = g * (SSM_HEADS // SSM_GROUPS) + hh
            hs = slice(h * SSM_HEAD_DIM, (h + 1) * SSM_HEAD_DIM)
            h_ref[hs, :] = h_ref[hs, :] * dec[:, h:h + 1] + st[hh * SSM_HEAD_DIM:(hh + 1) * SSM_HEAD_DIM, :]

    @pl.when(c == pl.num_programs(1) - 1)
    def _():
        hout_ref[...] = h_ref[...]


def _ssd_prompt(u, lw, nb, lp, l_valid):
    t = u.shape[0]
    nc = lp // CHUNK
    row = lambda b, i: b * nc + i
    full = lambda shape: pl.BlockSpec(shape, lambda b, i: (0,) * len(shape))
    return pl.pallas_call(
        functools.partial(_ssd_prompt_kernel, l_valid=l_valid),
        grid=(nb, nc),
        in_specs=[
            pl.BlockSpec((CHUNK, XBC_DIM), lambda b, i: (row(b, i), U_XBC // XBC_DIM)),
            pl.BlockSpec((CHUNK, SSM_DIM), lambda b, i: (row(b, i), U_Z // SSM_DIM)),
            pl.BlockSpec((CHUNK, LANE), lambda b, i: (row(b, i), U_DT // LANE)),
            full((SSM_CONV, XBC_DIM)), full((1, XBC_DIM)), full((1, LANE)), full((1, LANE)),
            full((1, SSM_DIM)), full((1, SSM_DIM)),
        ],
        out_specs=[
            pl.BlockSpec((CHUNK, SSM_DIM), lambda b, i: (row(b, i), 0)),
            pl.BlockSpec((None, SSM_DIM, SSM_STATE), lambda b, i: (b, 0, 0)),
        ],
        out_shape=[jax.ShapeDtypeStruct((t, SSM_DIM), _BF), jax.ShapeDtypeStruct((nb, SSM_DIM, SSM_STATE), _F32)],
        scratch_shapes=[
            pltpu.VMEM((CHUNK + SUBLANE, XBC_DIM), _F32), pltpu.VMEM((CHUNK, XBC_DIM), _F32),
            pltpu.VMEM((CHUNK, SSM_DIM), _F32), pltpu.VMEM((CHUNK, SSM_DIM), _F32),
            pltpu.VMEM((SSM_DIM, SSM_STATE), _F32),
        ],
        compiler_params=_cparams(("parallel", "arbitrary")),
        name="ssd_prompt",
    )(u, u, u, lw["scw"], lw["scb"], lw["dtb"], lw["a"], lw["dskip"], lw["sng"])


def _ssd_sample_kernel(xbc_ref, z_ref, dt_ref, left_ref, h0_ref, cw_ref, cb_ref, dtb_ref, a_ref, d_ref, ng_ref,
                       o_ref, hout_ref, xp_ref, xa_ref, y_ref, xw_ref, yo_ref):
    xp_ref[:, 0:SUBLANE, :] = left_ref[...]
    xp_ref[:, SUBLANE:2 * SUBLANE, :] = xbc_ref[...].reshape(REQ_BLK, SUBLANE, XBC_DIM)
    for c in range(XBC_DIM // LANE):
        cs = slice(c * LANE, (c + 1) * LANE)
        acc = jnp.zeros((REQ_BLK, SUBLANE, LANE), _F32) + cb_ref[:, cs][None]
        for k in range(SSM_CONV):
            off = SUBLANE - (SSM_CONV - 1) + k
            acc = acc + cw_ref[k:k + 1, cs][None] * xp_ref[:, off:off + SUBLANE, cs]
        xa_ref[:, cs] = _silu(acc).reshape(CHUNK, LANE)

    ri = lax.broadcasted_iota(jnp.int32, (CHUNK, CHUNK), 0)
    ci = lax.broadcasted_iota(jnp.int32, (CHUNK, CHUNK), 1)
    same = (ri >> 3) == (ci >> 3)
    mask = jnp.logical_and(same, ci <= ri)
    tmat = jnp.where(mask, 1.0, 0.0).astype(_BF)
    lastmat = jnp.where(same, 1.0, 0.0).astype(_BF)
    dt = _softplus(dt_ref[...] + dtb_ref[...])
    gw = SSM_DIM // SSM_GROUPS

    for r in range(REQ_BLK):
        rs = slice(r * SUBLANE, (r + 1) * SUBLANE)
        for g in range(SSM_GROUPS):
            cm = xa_ref[rs, SSM_DIM + (SSM_GROUPS + g) * SSM_STATE:SSM_DIM + (SSM_GROUPS + g + 1) * SSM_STATE]
            hb = h0_ref[r, g * gw:(g + 1) * gw, :].astype(_BF)
            yo_ref[rs, g * gw:(g + 1) * gw] = _dot_nt(cm.astype(_BF), hb)

    def hprev_fn(g, cm):
        return yo_ref[:, g * gw:(g + 1) * gw]

    acum, atot = _ssd_chunk(xa_ref, dt, a_ref[...], mask, tmat, hprev_fn, d_ref, y_ref, xw_ref, lastmat=lastmat)
    _ssd_gate_norm(y_ref[...], z_ref[...], ng_ref, o_ref)

    xw_t = xw_ref[...].T
    dec = jnp.exp(atot)
    lane_req = lax.broadcasted_iota(jnp.int32, (gw, CHUNK), 1) >> 3
    for g in range(SSM_GROUPS):
        bm = xa_ref[:, SSM_DIM + g * SSM_STATE:SSM_DIM + (g + 1) * SSM_STATE].astype(_BF)
        xg = xw_t[g * gw:(g + 1) * gw, :]
        for r in range(REQ_BLK):
            st = _dot(jnp.where(lane_req == r, xg, 0.0).astype(_BF), bm)
            for hh in range(SSM_HEADS // SSM_GROUPS):
                h = g * (SSM_HEADS // SSM_GROUPS) + hh
                hs = slice(h * SSM_HEAD_DIM, (h + 1) * SSM_HEAD_DIM)
                d = dec[r * SUBLANE:r * SUBLANE + 1, h:h + 1]
                hout_ref[r, hs, :] = h0_ref[r, hs, :] * d + st[hh * SSM_HEAD_DIM:(hh + 1) * SSM_HEAD_DIM, :]


def _ssd_sample(u, left8, h0_all, layer, lw):
    t = u.shape[0]
    nreq = t // SUBLANE
    full = lambda shape: pl.BlockSpec(shape, lambda i: (0,) * len(shape))
    return pl.pallas_call(
        _ssd_sample_kernel,
        grid=(nreq // REQ_BLK,),
        in_specs=[
            pl.BlockSpec((CHUNK, XBC_DIM), lambda i: (i, U_XBC // XBC_DIM)),
            pl.BlockSpec((CHUNK, SSM_DIM), lambda i: (i, U_Z // SSM_DIM)),
            pl.BlockSpec((CHUNK, LANE), lambda i: (i, U_DT // LANE)),
            pl.BlockSpec((REQ_BLK, SUBLANE, XBC_DIM), lambda i: (i, 0, 0)),
            pl.BlockSpec((None, REQ_BLK, SSM_DIM, SSM_STATE), lambda i: (layer, i, 0, 0)),
            full((SSM_CONV, XBC_DIM)), full((1, XBC_DIM)), full((1, LANE)), full((1, LANE)),
            full((1, SSM_DIM)), full((1, SSM_DIM)),
        ],
        out_specs=[
            pl.BlockSpec((CHUNK, SSM_DIM), lambda i: (i, 0)),
            pl.BlockSpec((REQ_BLK, SSM_DIM, SSM_STATE), lambda i: (i, 0, 0)),
        ],
        out_shape=[jax.ShapeDtypeStruct((t, SSM_DIM), _BF), jax.ShapeDtypeStruct((nreq, SSM_DIM, SSM_STATE), _F32)],
        scratch_shapes=[
            pltpu.VMEM((REQ_BLK, 2 * SUBLANE, XBC_DIM), _F32), pltpu.VMEM((CHUNK, XBC_DIM), _F32),
            pltpu.VMEM((CHUNK, SSM_DIM), _F32), pltpu.VMEM((CHUNK, SSM_DIM), _F32),
            pltpu.VMEM((CHUNK, SSM_DIM), _F32),
        ],
        compiler_params=_cparams(("parallel",)),
        name="ssd_sample",
    )(u, u, u, left8, h0_all, lw["scw"], lw["scb"], lw["dtb"], lw["a"], lw["dskip"], lw["sng"])


def _outproj_kernel(cy_ref, my_ref, sy_ref, h_ref, w_ref, o_ref, mix_ref):
    @pl.when(pl.program_id(1) == 0)
    def _():
        mix_ref[:, 0:CONV_DIM] = cy_ref[...].astype(_BF)
        mix_ref[:, CONV_DIM:CONV_DIM + MLA_DIM] = my_ref[...].astype(_BF)
        mix_ref[:, CONV_DIM + MLA_DIM:] = sy_ref[...].astype(_BF)

    o_ref[...] = h_ref[...] + _dot(mix_ref[...], w_ref[...])


def _outproj(cy, my, sy, h, w_all, layer, tm, tn=1024):
    t = h.shape[0]
    return pl.pallas_call(
        _outproj_kernel,
        grid=(t // tm, D_MODEL // tn),
        in_specs=[
            pl.BlockSpec((tm, CONV_DIM), lambda i, j: (i, 0)),
            pl.BlockSpec((tm, MLA_DIM), lambda i, j: (i, 0)),
            pl.BlockSpec((tm, SSM_DIM), lambda i, j: (i, 0)),
            pl.BlockSpec((tm, tn), lambda i, j: (i, j)),
            pl.BlockSpec((None, D_MODEL, tn), lambda i, j: (layer, 0, j)),
        ],
        out_specs=pl.BlockSpec((tm, tn), lambda i, j: (i, j)),
        out_shape=jax.ShapeDtypeStruct((t, D_MODEL), _F32),
        scratch_shapes=[pltpu.VMEM((tm, D_MODEL), _BF)],
        compiler_params=_cparams(("parallel", "arbitrary")),
        name="outproj",
    )(cy, my, sy, h, w_all)


def _final_norm_kernel(x_ref, g_ref, o_ref):
    x = x_ref[...]
    ms = jnp.mean(x * x, axis=-1, keepdims=True)
    o_ref[...] = x * lax.rsqrt(ms + EPS) * g_ref[...]


def _final_norm(h, g, tm):
    t = h.shape[0]
    return pl.pallas_call(
        _final_norm_kernel,
        grid=(t // tm,),
        in_specs=[pl.BlockSpec((tm, D_MODEL), lambda i: (i, 0)), pl.BlockSpec((1, D_MODEL), lambda i: (0, 0))],
        out_specs=pl.BlockSpec((tm, D_MODEL), lambda i: (i, 0)),
        out_shape=jax.ShapeDtypeStruct((t, D_MODEL), _F32),
        compiler_params=_cparams(("parallel",)),
        name="final_norm",
    )(h, g)


def _rot_half_cols(w):
    half = ROPE_DIM // 2
    return jnp.concatenate([-w[..., half:], w[..., :half]], axis=-1)


def _prep_weights(norm_g, w_in, conv_dw_w, conv_dw_b, conv_ln_g, conv_ln_b, conv_pw_w, conv_pw_b, mla_qa_g,
                  mla_wq_b, mla_kva_g, mla_wkv_b, ssm_conv_w, ssm_conv_b, ssm_dt_bias, ssm_A_log, ssm_D,
                  ssm_norm_g, w_out):
    o = 0
    parts = {}
    for name, width in (("glu", 2 * CONV_DIM), ("cg", CONV_DIM), ("cq", Q_LORA), ("ckv", KV_LORA),
                        ("kpe", ROPE_DIM), ("mg", MLA_DIM), ("z", SSM_DIM), ("xbc", XBC_DIM), ("dt", SSM_HEADS)):
        parts[name] = w_in[..., o:o + width]
        o += width
    dt_pad = jnp.pad(parts["dt"], ((0, 0), (0, 0), (0, LANE - SSM_HEADS)))
    w_in_p = jnp.concatenate(
        [parts["glu"], parts["mg"], parts["xbc"], parts["cg"], parts["cq"], parts["z"], parts["ckv"],
         parts["kpe"], _rot_half_cols(parts["kpe"]), dt_pad], axis=-1).astype(_BF)

    depth = w_in.shape[0]
    wq_nope = mla_wq_b[..., :NOPE_DIM].reshape(depth, Q_LORA, MLA_HEADS * NOPE_DIM)
    wq_rope = mla_wq_b[..., NOPE_DIM:]
    wq_slots = jnp.concatenate([wq_rope, _rot_half_cols(wq_rope)], axis=-1).reshape(depth, Q_LORA, MLA_HEADS * LANE)
    wq = jnp.concatenate([wq_nope, wq_slots], axis=-1).astype(_BF)
    wuk = jnp.transpose(mla_wkv_b[..., :NOPE_DIM], (0, 2, 3, 1)).astype(_BF)
    wqt = jnp.swapaxes(wq, 1, 2)
    wukt = jnp.transpose(mla_wkv_b[..., :NOPE_DIM], (0, 2, 1, 3)).astype(_BF)
    wuvt = jnp.transpose(mla_wkv_b[..., NOPE_DIM:], (0, 2, 3, 1)).astype(_BF)
    wuv_all = mla_wkv_b[..., NOPE_DIM:].reshape(depth, KV_LORA, MLA_DIM).astype(_BF)

    def row(x):
        return x[:, None, :].astype(_F32)

    pad_heads = lambda x: jnp.pad(x, ((0, 0), (0, LANE - SSM_HEADS)))
    layers = []
    for l in range(depth):
        layers.append({
            "norm_g": row(norm_g)[l],
            "dww": conv_dw_w[l], "dwb": row(conv_dw_b)[l], "lng": row(conv_ln_g)[l], "lnb": row(conv_ln_b)[l],
            "pww": conv_pw_w[l].astype(_BF), "pwb": row(conv_pw_b)[l],
            "qag": row(mla_qa_g)[l], "wq": wq[l], "wuk": wuk[l], "kvg": row(mla_kva_g)[l],
            "wqt": wqt[l], "wukt": wukt[l], "wuvt": wuvt[l], "wuv_all": wuv_all[l],
            "scw": ssm_conv_w[l], "scb": row(ssm_conv_b)[l],
            "dtb": row(pad_heads(ssm_dt_bias))[l], "a": row(pad_heads(-jnp.exp(ssm_A_log.astype(_F32))))[l],
            "dskip": row(jnp.repeat(ssm_D, SSM_HEAD_DIM, axis=-1))[l], "sng": row(ssm_norm_g)[l],
        })
    return layers, w_in_p, w_out.astype(_BF)


def _rope_table(pos):
    inv = ROPE_THETA ** (-jnp.arange(0, ROPE_DIM, 2, dtype=_F32) / ROPE_DIM)
    ang = pos.astype(_F32)[:, None] * inv[None, :]
    c, s = jnp.cos(ang), jnp.sin(ang)
    return jnp.concatenate([c, c, s, s], axis=-1)


def _largest_tile(t, cands):
    for c in cands:
        if t % c == 0:
            return c
    raise ValueError(f"no row tile for {t}")


def kernel(x_prompt, x_sample, cache_ckv, cache_kpe, page_table, state_conv, state_ssm_conv, state_ssm, meta_tokens, norm_g, w_in, conv_dw_w, conv_dw_b, conv_ln_g, conv_ln_b, conv_pw_w, conv_pw_b, mla_qa_g, mla_wq_b, mla_kva_g, mla_wkv_b, ssm_conv_w, ssm_conv_b, ssm_dt_bias, ssm_A_log, ssm_D, ssm_norm_g, w_out, final_g):
    nb, seq, _ = x_prompt.shape
    nreq, dseq, _ = x_sample.shape
    depth = w_in.shape[0]
    n_pages = page_table.shape[1]
    page_size = cache_ckv.shape[2]
    assert dseq == SUBLANE and nreq % REQ_BLK == 0 and page_size == LANE
    l_valid = N_META + seq
    lp = -(-l_valid // ATT_BLK) * ATT_BLK
    tp, ts = nb * lp, nreq * dseq

    layers, w_in_all, w_out_all = _prep_weights(
        norm_g, w_in, conv_dw_w, conv_dw_b, conv_ln_g, conv_ln_b, conv_pw_w, conv_pw_b, mla_qa_g, mla_wq_b,
        mla_kva_g, mla_wkv_b, ssm_conv_w, ssm_conv_b, ssm_dt_bias, ssm_A_log, ssm_D, ssm_norm_g, w_out)

    meta = jnp.broadcast_to(meta_tokens[None].astype(_F32), (nb, N_META, D_MODEL))
    hp = jnp.concatenate([meta, x_prompt, jnp.zeros((nb, lp - l_valid, D_MODEL), _F32)], axis=1).reshape(tp, D_MODEL)
    hs = x_sample.reshape(ts, D_MODEL)
    cs_p = jnp.tile(_rope_table(jnp.arange(lp, dtype=jnp.int32)), (nb, 1))
    cst_p = cs_p.T
    past_len = n_pages * page_size
    cs_s = jnp.tile(_rope_table(past_len + jnp.arange(dseq, dtype=jnp.int32)), (nreq, 1))
    pt_flat = page_table.reshape(-1).astype(jnp.int32)
    cache_kpe_t = jnp.swapaxes(cache_kpe, 2, 3)
    left_conv = jnp.pad(state_conv, ((0, 0), (0, 0), (2, 0), (0, 0)))
    left_ssm = jnp.pad(state_ssm_conv, ((0, 0), (0, 0), (SUBLANE - (SSM_CONV - 1), 0), (0, 0)))
    h0_all = state_ssm.reshape(depth, nreq, SSM_DIM, SSM_STATE)

    tm_p = _largest_tile(tp, (768, 512, 384, 256, 128))
    tm_s = _largest_tile(ts, (512, 256, 128))
    tq_p = _largest_tile(tp, (384, 256, 128))
    tq_s = _largest_tile(ts, (256, 128))

    st_p = [[] for _ in range(5)]
    st_s = [[] for _ in range(5)]
    for l in range(depth):
        lw = layers[l]
        u = _inproj(hp, lw["norm_g"], w_in_all, l, tm_p)
        v, cy = _conv_prompt(u, lw, nb, lp)
        qt, kcat, vt, ckv, kpe = _qkt(u, cs_p, cst_p, lw, nb, lp)
        my = _attn_prompt(qt, kcat, vt, u, lw["wuvt"], nb, lp)
        sy, hfin = _ssd_prompt(u, lw, nb, lp, l_valid)
        hp = _outproj(cy, my, sy, hp, w_out_all, l, tm_p)
        u3 = u.reshape(nb, lp, U_TOTAL)
        st_p[0].append(ckv.reshape(nb, lp, KV_LORA)[:, :l_valid])
        st_p[1].append(kpe.reshape(nb, lp, ROPE_DIM)[:, :l_valid])
        st_p[2].append(v.reshape(nb, lp, CONV_DIM)[:, l_valid - (CONV_WIDTH - 1):l_valid])
        st_p[3].append(u3[:, l_valid - (SSM_CONV - 1):l_valid, U_XBC:U_XBC + XBC_DIM])
        st_p[4].append(hfin.reshape(nb, SSM_HEADS, SSM_HEAD_DIM, SSM_STATE))
        u = _inproj(hs, lw["norm_g"], w_in_all, l, tm_s)
        cst, cy = _conv_sample(u, left_conv[l], lw)
        q, kcat, ckv, kpe = _qk(u, cs_s, lw, tq_s, _F32)
        my = _attn_sample(q, kcat, u, lw["wuv_all"], cache_ckv, cache_kpe_t, pt_flat, l, n_pages)
        sy, hnew = _ssd_sample(u, left_ssm[l], h0_all, l, lw)
        hs = _outproj(cy, my, sy, hs, w_out_all, l, tm_s)
        u3 = u.reshape(nreq, dseq, U_TOTAL)
        st_s[0].append(ckv.reshape(nreq, dseq, KV_LORA))
        st_s[1].append(kpe.reshape(nreq, dseq, ROPE_DIM))
        st_s[2].append(cst[:, 2:, :])
        st_s[3].append(u3[:, dseq - (SSM_CONV - 1):, U_XBC:U_XBC + XBC_DIM])
        st_s[4].append(hnew.reshape(nreq, SSM_HEADS, SSM_HEAD_DIM, SSM_STATE))

    y_prompt = _final_norm(hp, final_g[None, :].astype(_F32), tm_p).reshape(nb, lp, D_MODEL)[:, N_META:l_valid]
    y_sample = _final_norm(hs, final_g[None, :].astype(_F32), tm_s).reshape(nreq, dseq, D_MODEL)
    outs_p = [jnp.stack(s) for s in st_p]
    outs_s = [jnp.stack(s) for s in st_s]
    return (y_prompt, y_sample, *outs_p, *outs_s)
```

```python
import functools
import math

import jax
import jax.numpy as jnp
from jax import lax
from jax.experimental import pallas as pl
from jax.experimental.pallas import tpu as pltpu

D_MODEL = 2048
DEPTH = 4
N_META = 16
CONV_DIM = 512
CONV_WIDTH = 31
MLA_HEADS = 8
MLA_V_DIM = 128
MLA_DIM = MLA_HEADS * MLA_V_DIM
Q_LORA = 512
KV_LORA = 256
NOPE_DIM = 128
ROPE_DIM = 64
ROPE_THETA = 10000.0
MLA_SCALE = 1.0 / math.sqrt(NOPE_DIM + ROPE_DIM)
SSM_DIM = 512
SSM_HEAD_DIM = 64
SSM_HEADS = 8
SSM_GROUPS = 2
SSM_STATE = 128
SSM_CONV = 4
XBC_DIM = SSM_DIM + 2 * SSM_GROUPS * SSM_STATE
EPS = 1e-6

LANE = 128
SUBLANE = 8
VMEM_LIMIT = 48 * 1024 * 1024

U_GLU = 0
U_MG = 1024
U_XBC = 2048
U_CG = 3072
U_CQ = 3584
U_Z = 4096
U_CKV = 4608
U_KPE = 4864
U_DT = 4992
U_TOTAL = 5120

QK_PAD = 384
CHUNK = 128
ATT_BLK = 256
EXP2_SCALE = MLA_SCALE * math.log2(math.e)
_BF = jnp.bfloat16
_F32 = jnp.float32


def _cparams(sem):
    return pltpu.CompilerParams(dimension_semantics=sem, vmem_limit_bytes=VMEM_LIMIT)


def _dot(a, b):
    return jnp.dot(a, b, preferred_element_type=_F32)


def _dot_nt(a, b):
    return lax.dot_general(a, b, (((1,), (1,)), ((), ())), preferred_element_type=_F32)


def _sigmoid(x):
    return 1.0 / (1.0 + jnp.exp(-x))


def _silu(x):
    return x * _sigmoid(x)


def _softplus(x):
    return jnp.maximum(x, 0.0) + jnp.log(1.0 + jnp.exp(-jnp.abs(x)))


def _exact_dot_01(mat01_bf16, x):
    hi = x.astype(_BF)
    r1 = x - hi.astype(_F32)
    mid = r1.astype(_BF)
    lo = (r1 - mid.astype(_F32)).astype(_BF)
    return _dot(mat01_bf16, hi) + _dot(mat01_bf16, mid) + _dot(mat01_bf16, lo)


def _inproj_kernel(x_ref, g_ref, w_ref, o_ref, xn_ref):
    @pl.when(pl.program_id(1) == 0)
    def _():
        x = x_ref[...]
        ms = jnp.mean(x * x, axis=-1, keepdims=True)
        xn_ref[...] = (x * lax.rsqrt(ms + EPS) * g_ref[...]).astype(_BF)

    o_ref[...] = _dot_nt(xn_ref[...], w_ref[...])


def _inproj(h, g, w_all, layer, tm, tn=U_TOTAL // 2):
    t = h.shape[0]
    n = w_all.shape[1]
    return pl.pallas_call(
        _inproj_kernel,
        grid=(t // tm, n // tn),
        in_specs=[
            pl.BlockSpec((tm, D_MODEL), lambda i, j: (i, 0)),
            pl.BlockSpec((1, D_MODEL), lambda i, j: (0, 0)),
            pl.BlockSpec((None, tn, D_MODEL), lambda i, j: (layer, j, 0)),
        ],
        out_specs=pl.BlockSpec((tm, tn), lambda i, j: (i, j)),
        out_shape=jax.ShapeDtypeStruct((t, n), _F32),
        scratch_shapes=[pltpu.VMEM((tm, D_MODEL), _BF)],
        compiler_params=_cparams(("parallel", "arbitrary")),
        name="inproj",
    )(h, g, w_all)


def _conv_tail(y, cg, lng_ref, lnb_ref, pww_ref, pwb_ref):
    mu = jnp.mean(y, axis=-1, keepdims=True)
    yc = y - mu
    var = jnp.mean(yc * yc, axis=-1, keepdims=True)
    yn = yc * lax.rsqrt(var + EPS) * lng_ref[...] + lnb_ref[...]
    s = _silu(yn)
    o = _dot(s.astype(_BF), pww_ref[...]) + pwb_ref[...]
    return o * _silu(cg)


def _conv_prompt_kernel(glu_ref, cg_ref, dww_ref, dwb_ref, lng_ref, lnb_ref, pww_ref, pwb_ref,
                        v_ref, cy_ref, vp_ref, y_ref, sh_ref):
    @pl.when(pl.program_id(1) == 0)
    def _():
        vp_ref[0:32, :] = jnp.zeros((32, CONV_DIM), _F32)

    v = glu_ref[:, :CONV_DIM] * _sigmoid(glu_ref[:, CONV_DIM:])
    v_ref[...] = v
    vp_ref[32:32 + CHUNK, :] = v
    for c in range(CONV_DIM // LANE):
        cs = slice(c * LANE, (c + 1) * LANE)
        acc = jnp.zeros((CHUNK, LANE), _F32)
        for r in range(SUBLANE):
            taps = range(r, CONV_WIDTH, SUBLANE)
            span = CHUNK + SUBLANE * (len(taps) - 1)
            sh_ref[0:span, :] = vp_ref[2 + r:2 + r + span, cs]
            for j, k in enumerate(taps):
                acc = acc + dww_ref[k:k + 1, cs] * sh_ref[SUBLANE * j:SUBLANE * j + CHUNK, :]
        y_ref[:, cs] = acc + dwb_ref[:, cs]
    cy = _conv_tail(y_ref[...], cg_ref[...], lng_ref, lnb_ref, pww_ref, pwb_ref)
    cy_ref[...] = cy.astype(cy_ref.dtype)
    vp_ref[0:32, :] = vp_ref[CHUNK:CHUNK + 32, :]


def _conv_prompt(u, lw, nb, lp):
    t = u.shape[0]
    nt = lp // CHUNK
    row = lambda b, i: b * nt + i
    full = lambda shape: pl.BlockSpec(shape, lambda b, i: (0,) * len(shape))
    return pl.pallas_call(
        _conv_prompt_kernel,
        grid=(nb, nt),
        in_specs=[
            pl.BlockSpec((CHUNK, 2 * CONV_DIM), lambda b, i: (row(b, i), U_GLU // (2 * CONV_DIM))),
            pl.BlockSpec((CHUNK, CONV_DIM), lambda b, i: (row(b, i), U_CG // CONV_DIM)),
            full((CONV_WIDTH, CONV_DIM)), full((1, CONV_DIM)), full((1, CONV_DIM)), full((1, CONV_DIM)),
            full((CONV_DIM, CONV_DIM)), full((1, CONV_DIM)),
        ],
        out_specs=[
            pl.BlockSpec((CHUNK, CONV_DIM), lambda b, i: (row(b, i), 0)),
            pl.BlockSpec((CHUNK, CONV_DIM), lambda b, i: (row(b, i), 0)),
        ],
        out_shape=[jax.ShapeDtypeStruct((t, CONV_DIM), _F32), jax.ShapeDtypeStruct((t, CONV_DIM), _BF)],
        scratch_shapes=[pltpu.VMEM((CHUNK + 32, CONV_DIM), _F32), pltpu.VMEM((CHUNK, CONV_DIM), _F32),
                        pltpu.VMEM((CHUNK + 32, LANE), _F32)],
        compiler_params=_cparams(("parallel", "arbitrary")),
        name="conv_prompt",
    )(u, u, lw["dww"], lw["dwb"], lw["lng"], lw["lnb"], lw["pww"], lw["pwb"])


REQ_BLK = CHUNK // SUBLANE


def _conv_sample_kernel(glu_ref, cg_ref, left_ref, dww_ref, dwb_ref, lng_ref, lnb_ref, pww_ref, pwb_ref,
                        st_ref, cy_ref, vp_ref, y_ref):
    v = glu_ref[:, :CONV_DIM] * _sigmoid(glu_ref[:, CONV_DIM:])
    vp_ref[:, 0:32, :] = left_ref[...]
    vp_ref[:, 32:40, :] = v.reshape(REQ_BLK, SUBLANE, CONV_DIM)
    for c in range(CONV_DIM // LANE):
        cs = slice(c * LANE, (c + 1) * LANE)
        acc = jnp.zeros((REQ_BLK, SUBLANE, LANE), _F32)
        for k in range(CONV_WIDTH):
            acc = acc + dww_ref[k:k + 1, cs][None] * vp_ref[:, 2 + k:2 + k + SUBLANE, cs]
        y_ref[:, :, cs] = acc + dwb_ref[:, cs][None]
    y = y_ref[...].reshape(CHUNK, CONV_DIM)
    cy = _conv_tail(y, cg_ref[...], lng_ref, lnb_ref, pww_ref, pwb_ref)
    cy_ref[...] = cy.astype(cy_ref.dtype)
    st_ref[...] = vp_ref[:, 8:40, :]


def _conv_sample(u, left32, lw):
    t = u.shape[0]
    nreq = t // SUBLANE
    full = lambda shape: pl.BlockSpec(shape, lambda i: (0,) * len(shape))
    return pl.pallas_call(
        _conv_sample_kernel,
        grid=(nreq // REQ_BLK,),
        in_specs=[
            pl.BlockSpec((CHUNK, 2 * CONV_DIM), lambda i: (i, U_GLU // (2 * CONV_DIM))),
            pl.BlockSpec((CHUNK, CONV_DIM), lambda i: (i, U_CG // CONV_DIM)),
            pl.BlockSpec((REQ_BLK, 32, CONV_DIM), lambda i: (i, 0, 0)),
            full((CONV_WIDTH, CONV_DIM)), full((1, CONV_DIM)), full((1, CONV_DIM)), full((1, CONV_DIM)),
            full((CONV_DIM, CONV_DIM)), full((1, CONV_DIM)),
        ],
        out_specs=[
            pl.BlockSpec((REQ_BLK, 32, CONV_DIM), lambda i: (i, 0, 0)),
            pl.BlockSpec((CHUNK, CONV_DIM), lambda i: (i, 0)),
        ],
        out_shape=[jax.ShapeDtypeStruct((nreq, 32, CONV_DIM), _F32), jax.ShapeDtypeStruct((t, CONV_DIM), _BF)],
        scratch_shapes=[pltpu.VMEM((REQ_BLK, 40, CONV_DIM), _F32), pltpu.VMEM((REQ_BLK, SUBLANE, CONV_DIM), _F32)],
        compiler_params=_cparams(("parallel",)),
        name="conv_sample",
    )(u, u, left32, lw["dww"], lw["dwb"], lw["lng"], lw["lnb"], lw["pww"], lw["pwb"])


def _rope_slot(slot, cs):
    r = slot * cs
    r = r + pltpu.roll(r, 64, axis=1)
    lane = lax.broadcasted_iota(jnp.int32, r.shape, 1)
    return jnp.where(lane < ROPE_DIM, r, 0.0)


def _qk_kernel(cq_ref, ckv_ref, kpe_ref, cs_ref, qag_ref, wq_ref, wuk_ref, kvg_ref,
               q_ref, kcat_ref, ckvo_ref, kpeo_ref):
    cq = cq_ref[...]
    ms = jnp.mean(cq * cq, axis=-1, keepdims=True)
    cqn = (cq * lax.rsqrt(ms + EPS) * qag_ref[...]).astype(_BF)
    q = _dot(cqn, wq_ref[...])
    cs = cs_ref[...]
    for h in range(MLA_HEADS):
        qn = q[:, h * NOPE_DIM:(h + 1) * NOPE_DIM].astype(_BF)
        ql = _dot(qn, wuk_ref[h])
        q_ref[h, :, 0:KV_LORA] = ql.astype(q_ref.dtype)
        slot = q[:, MLA_HEADS * NOPE_DIM + h * LANE:MLA_HEADS * NOPE_DIM + (h + 1) * LANE]
        q_ref[h, :, KV_LORA:QK_PAD] = _rope_slot(slot, cs).astype(q_ref.dtype)
    c = ckv_ref[...]
    ms = jnp.mean(c * c, axis=-1, keepdims=True)
    ckv = c * lax.rsqrt(ms + EPS) * kvg_ref[...]
    ckvo_ref[...] = ckv
    kcat_ref[:, 0:KV_LORA] = ckv.astype(kcat_ref.dtype)
    kr = _rope_slot(kpe_ref[...], cs)
    kcat_ref[:, KV_LORA:QK_PAD] = kr.astype(kcat_ref.dtype)
    kpeo_ref[...] = kr[:, :ROPE_DIM]


def _qk(u, cs, lw, tm, qdtype):
    t = u.shape[0]
    full = lambda shape: pl.BlockSpec(shape, lambda i: (0,) * len(shape))
    return pl.pallas_call(
        _qk_kernel,
        grid=(t // tm,),
        in_specs=[
            pl.BlockSpec((tm, Q_LORA), lambda i: (i, U_CQ // Q_LORA)),
            pl.BlockSpec((tm, KV_LORA), lambda i: (i, U_CKV // KV_LORA)),
            pl.BlockSpec((tm, LANE), lambda i: (i, U_KPE // LANE)),
            pl.BlockSpec((tm, LANE), lambda i: (i, 0)),
            full((1, Q_LORA)), full((Q_LORA, 2 * MLA_HEADS * NOPE_DIM)),
            full((MLA_HEADS, NOPE_DIM, KV_LORA)), full((1, KV_LORA)),
        ],
        out_specs=[
            pl.BlockSpec((MLA_HEADS, tm, QK_PAD), lambda i: (0, i, 0)),
            pl.BlockSpec((tm, QK_PAD), lambda i: (i, 0)),
            pl.BlockSpec((tm, KV_LORA), lambda i: (i, 0)),
            pl.BlockSpec((tm, ROPE_DIM), lambda i: (i, 0)),
        ],
        out_shape=[
            jax.ShapeDtypeStruct((MLA_HEADS, t, QK_PAD), qdtype),
            jax.ShapeDtypeStruct((t, QK_PAD), qdtype),
            jax.ShapeDtypeStruct((t, KV_LORA), _F32),
            jax.ShapeDtypeStruct((t, ROPE_DIM), _F32),
        ],
        compiler_params=_cparams(("parallel",)),
        name="qk",
    )(u, u, u, cs, lw["qag"], lw["wq"], lw["wuk"], lw["kvg"])


def _qkt_kernel(cq_ref, ckv_ref, kpe_ref, cs_ref, cst_ref, qag_ref, wqt_ref, wukt_ref, kvg_ref,
                qt_ref, kcat_ref, vt_ref, ckvo_ref, kpeo_ref):
    tm = ATT_BLK
    cq = cq_ref[...]
    ms = jnp.mean(cq * cq, axis=-1, keepdims=True)
    cqn_t = (cq * lax.rsqrt(ms + EPS) * qag_ref[...]).T.astype(_BF)
    qt = _dot(wqt_ref[...], cqn_t)
    cst = cst_ref[...]
    for h in range(MLA_HEADS):
        cols = slice(h * tm, (h + 1) * tm)
        qn_t = qt[h * NOPE_DIM:(h + 1) * NOPE_DIM].astype(_BF)
        qt_ref[0:KV_LORA, cols] = _dot(wukt_ref[h], qn_t).astype(qt_ref.dtype)
        r = qt[MLA_HEADS * NOPE_DIM + h * LANE:MLA_HEADS * NOPE_DIM + (h + 1) * LANE] * cst
        qt_ref[KV_LORA:KV_LORA + ROPE_DIM, cols] = (r[:ROPE_DIM] + r[ROPE_DIM:]).astype(qt_ref.dtype)
        qt_ref[KV_LORA + ROPE_DIM:QK_PAD, cols] = jnp.zeros((QK_PAD - KV_LORA - ROPE_DIM, tm), qt_ref.dtype)
    c = ckv_ref[...]
    ms = jnp.mean(c * c, axis=-1, keepdims=True)
    ckv = c * lax.rsqrt(ms + EPS) * kvg_ref[...]
    ckvo_ref[...] = ckv
    kcat_ref[:, 0:KV_LORA] = ckv.astype(kcat_ref.dtype)
    vt_ref[...] = ckv.T.astype(vt_ref.dtype)
    kr = _rope_slot(kpe_ref[...], cs_ref[...])
    kcat_ref[:, KV_LORA:QK_PAD] = kr.astype(kcat_ref.dtype)
    kpeo_ref[...] = kr[:, :ROPE_DIM]


def _qkt(u, cs, cst, lw, nb, lp):
    t = u.shape[0]
    tm = ATT_BLK
    nblk = t // tm
    full = lambda shape: pl.BlockSpec(shape, lambda i: (0,) * len(shape))
    return pl.pallas_call(
        _qkt_kernel,
        grid=(nblk,),
        in_specs=[
            pl.BlockSpec((tm, Q_LORA), lambda i: (i, U_CQ // Q_LORA)),
            pl.BlockSpec((tm, KV_LORA), lambda i: (i, U_CKV // KV_LORA)),
            pl.BlockSpec((tm, LANE), lambda i: (i, U_KPE // LANE)),
            pl.BlockSpec((tm, LANE), lambda i: (i, 0)),
            pl.BlockSpec((LANE, tm), lambda i: (0, i)),
            full((1, Q_LORA)), full((2 * MLA_HEADS * NOPE_DIM, Q_LORA)),
            full((MLA_HEADS, KV_LORA, NOPE_DIM)), full((1, KV_LORA)),
        ],
        out_specs=[
            pl.BlockSpec((None, QK_PAD, MLA_HEADS * tm), lambda i: (i, 0, 0)),
            pl.BlockSpec((tm, QK_PAD), lambda i: (i, 0)),
            pl.BlockSpec((None, KV_LORA, tm), lambda i: (i, 0, 0)),
            pl.BlockSpec((tm, KV_LORA), lambda i: (i, 0)),
            pl.BlockSpec((tm, ROPE_DIM), lambda i: (i, 0)),
        ],
        out_shape=[
            jax.ShapeDtypeStruct((nblk, QK_PAD, MLA_HEADS * tm), _BF),
            jax.ShapeDtypeStruct((t, QK_PAD), _BF),
            jax.ShapeDtypeStruct((nblk, KV_LORA, tm), _BF),
            jax.ShapeDtypeStruct((t, KV_LORA), _F32),
            jax.ShapeDtypeStruct((t, ROPE_DIM), _F32),
        ],
        compiler_params=_cparams(("parallel",)),
        name="qkt",
    )(u, u, u, cs, cst, lw["qag"], lw["wqt"], lw["wukt"], lw["kvg"])


def _attn_prompt_kernel(qt_ref, k_ref, vt_ref, mg_ref, wuvt_ref, o_ref, m_ref, l_ref, acc_ref):
    qi = pl.program_id(1)
    ncol = MLA_HEADS * ATT_BLK
    kk = lax.broadcasted_iota(jnp.int32, (ATT_BLK, ATT_BLK), 0)
    qq = lax.broadcasted_iota(jnp.int32, (ATT_BLK, ATT_BLK), 1)
    diag_mask = jnp.concatenate([kk <= qq] * MLA_HEADS, axis=1)

    m_ref[...] = jnp.full((1, ncol), -jnp.inf, _F32)
    l_ref[...] = jnp.zeros((1, ncol), _F32)
    acc_ref[...] = jnp.zeros((KV_LORA, ncol), _F32)

    def chunk(c, masked):
        kc = k_ref[pl.ds(pl.multiple_of(c * ATT_BLK, ATT_BLK), ATT_BLK), :]
        s = _dot(kc, qt_ref[...])
        if masked:
            s = jnp.where(diag_mask, s, -jnp.inf)
        m_prev = m_ref[...]
        m_new = jnp.maximum(m_prev, jnp.max(s, axis=0, keepdims=True))
        alpha = jnp.exp2((m_prev - m_new) * EXP2_SCALE)
        p = jnp.exp2((s - m_new) * EXP2_SCALE)
        l_ref[...] = alpha * l_ref[...] + jnp.sum(p, axis=0, keepdims=True)
        acc_ref[...] = alpha * acc_ref[...] + _dot(vt_ref[c], p.astype(_BF))
        m_ref[...] = m_new

    def body(c, carry):
        chunk(c, False)
        return carry

    lax.fori_loop(0, qi, body, 0)
    chunk(qi, True)
    for h in range(MLA_HEADS):
        cols = slice(h * ATT_BLK, (h + 1) * ATT_BLK)
        o_t = (acc_ref[:, cols] / l_ref[:, cols]).astype(_BF)
        y = _dot(wuvt_ref[h], o_t).T
        g = mg_ref[:, h * MLA_V_DIM:(h + 1) * MLA_V_DIM]
        o_ref[:, h * MLA_V_DIM:(h + 1) * MLA_V_DIM] = (y * _silu(g)).astype(o_ref.dtype)


def _attn_prompt(qt, kcat, vt, u, wuvt, nb, lp):
    t = kcat.shape[0]
    nq = lp // ATT_BLK
    return pl.pallas_call(
        _attn_prompt_kernel,
        grid=(nb, nq),
        in_specs=[
            pl.BlockSpec((None, QK_PAD, MLA_HEADS * ATT_BLK), lambda b, i: (b * nq + i, 0, 0)),
            pl.BlockSpec((None, lp, QK_PAD), lambda b, i: (b, 0, 0)),
            pl.BlockSpec((None, nq, KV_LORA, ATT_BLK), lambda b, i: (b, 0, 0, 0)),
            pl.BlockSpec((ATT_BLK, MLA_DIM), lambda b, i: (b * nq + i, U_MG // MLA_DIM)),
            pl.BlockSpec((MLA_HEADS, MLA_V_DIM, KV_LORA), lambda b, i: (0, 0, 0)),
        ],
        out_specs=pl.BlockSpec((ATT_BLK, MLA_DIM), lambda b, i: (b * nq + i, 0)),
        out_shape=jax.ShapeDtypeStruct((t, MLA_DIM), _BF),
        scratch_shapes=[
            pltpu.VMEM((1, MLA_HEADS * ATT_BLK), _F32),
            pltpu.VMEM((1, MLA_HEADS * ATT_BLK), _F32),
            pltpu.VMEM((KV_LORA, MLA_HEADS * ATT_BLK), _F32),
        ],
        compiler_params=_cparams(("parallel", "arbitrary")),
        name="attn_prompt",
    )(qt, kcat.reshape(nb, lp, QK_PAD), vt.reshape(nb, nq, KV_LORA, ATT_BLK), u, wuvt)


def _page_copies(pt_ref, ckv_hbm, kpe_hbm, ckv_buf, kpe_buf, sem, req, slot, layer, n_pages):
    copies = []
    for i in range(n_pages):
        pg = pt_ref[req * n_pages + i]
        copies.append(pltpu.make_async_copy(
            ckv_hbm.at[layer, pg], ckv_buf.at[slot, pl.ds(i * LANE, LANE), :], sem.at[0, slot]))
        copies.append(pltpu.make_async_copy(kpe_hbm.at[layer, pg], kpe_buf.at[slot, i], sem.at[1, slot]))
    return copies


def _attn_sample_kernel(pt_ref, q_ref, kn_ref, mg_ref, wuv_ref, ckv_hbm, kpe_hbm, o_ref,
                        ckv_buf, kpe_buf, kbf_ref, sem, *, layer, n_pages):
    b = pl.program_id(0)
    slot = b & 1
    rows = MLA_HEADS * SUBLANE
    fetch = functools.partial(_page_copies, pt_ref, ckv_hbm, kpe_hbm, ckv_buf, kpe_buf, sem,
                              layer=layer, n_pages=n_pages)

    @pl.when(b == 0)
    def _():
        for c in fetch(req=b, slot=slot):
            c.start()

    @pl.when(b + 1 < pl.num_programs(0))
    def _():
        for c in fetch(req=b + 1, slot=1 - slot):
            c.start()

    for c in fetch(req=b, slot=slot):
        c.wait()

    q = q_ref[...].reshape(rows, QK_PAD)
    qb = q.astype(_BF)
    kbf_ref[...] = ckv_buf[slot].astype(_BF)
    kpt = jnp.concatenate([kpe_buf[slot, i] for i in range(n_pages)], axis=1).astype(_BF)
    s = (_dot_nt(qb[:, :KV_LORA], kbf_ref[...]) + _dot(qb[:, KV_LORA:KV_LORA + ROPE_DIM], kpt)) * MLA_SCALE
    kn = jnp.concatenate([kn_ref[...], jnp.zeros((LANE - SUBLANE, QK_PAD), _F32)], axis=0).astype(_BF)
    sn = _dot_nt(qb, kn) * MLA_SCALE
    tq = lax.broadcasted_iota(jnp.int32, sn.shape, 0) & (SUBLANE - 1)
    tk = lax.broadcasted_iota(jnp.int32, sn.shape, 1)
    sn = jnp.where(tk <= tq, sn, -jnp.inf)
    m = jnp.maximum(jnp.max(s, axis=-1, keepdims=True), jnp.max(sn, axis=-1, keepdims=True))
    p = jnp.exp(s - m)
    pn = jnp.exp(sn - m)
    l = jnp.sum(p, axis=-1, keepdims=True) + jnp.sum(pn, axis=-1, keepdims=True)
    acc = _dot(p.astype(_BF), kbf_ref[...]) + _dot(pn.astype(_BF), kn[:, :KV_LORA])
    o = (acc / l).astype(_BF)
    y = _dot(o, wuv_ref[...])
    mg = mg_ref[...]
    for h in range(MLA_HEADS):
        cs = slice(h * MLA_V_DIM, (h + 1) * MLA_V_DIM)
        yh = y[h * SUBLANE:(h + 1) * SUBLANE, cs]
        o_ref[:, cs] = (yh * _silu(mg[:, cs])).astype(o_ref.dtype)


def _attn_sample(q, kcat, u, wuv_all, cache_ckv, cache_kpe_t, pt_flat, layer, n_pages):
    t = kcat.shape[0]
    nreq = t // SUBLANE
    nkeys = n_pages * LANE
    grid_spec = pltpu.PrefetchScalarGridSpec(
        num_scalar_prefetch=1,
        grid=(nreq,),
        in_specs=[
            pl.BlockSpec((MLA_HEADS, SUBLANE, QK_PAD), lambda b, pt: (0, b, 0)),
            pl.BlockSpec((SUBLANE, QK_PAD), lambda b, pt: (b, 0)),
            pl.BlockSpec((SUBLANE, MLA_DIM), lambda b, pt: (b, U_MG // MLA_DIM)),
            pl.BlockSpec((KV_LORA, MLA_DIM), lambda b, pt: (0, 0)),
            pl.BlockSpec(memory_space=pl.ANY),
            pl.BlockSpec(memory_space=pl.ANY),
        ],
        out_specs=pl.BlockSpec((SUBLANE, MLA_DIM), lambda b, pt: (b, 0)),
        scratch_shapes=[
            pltpu.VMEM((2, nkeys, KV_LORA), _F32),
            pltpu.VMEM((2, n_pages, ROPE_DIM, LANE), _F32),
            pltpu.VMEM((nkeys, KV_LORA), _BF),
            pltpu.SemaphoreType.DMA((2, 2)),
        ],
    )
    return pl.pallas_call(
        functools.partial(_attn_sample_kernel, layer=layer, n_pages=n_pages),
        grid_spec=grid_spec,
        out_shape=jax.ShapeDtypeStruct((t, MLA_DIM), _F32),
        compiler_params=_cparams(("arbitrary",)),
        name="attn_sample",
    )(pt_flat, q, kcat, u, wuv_all, cache_ckv, cache_kpe_t)


def _ssd_chunk(xa_ref, dt, a_row, mask, tmat, hprev_fn, d_ref, y_ref, xw_ref, lastmat=None):
    da = dt * a_row
    acum = _exact_dot_01(tmat, da)
    atot = acum[CHUNK - 1:CHUNK, :] if lastmat is None else _exact_dot_01(lastmat, da)
    acum_t = acum.T
    dt_t = dt.T
    ea = jnp.exp(acum)
    w = jnp.exp(atot - acum) * dt
    heads_per_group = SSM_HEADS // SSM_GROUPS
    for g in range(SSM_GROUPS):
        bm = xa_ref[:, SSM_DIM + g * SSM_STATE:SSM_DIM + (g + 1) * SSM_STATE].astype(_BF)
        cm = xa_ref[:, SSM_DIM + (SSM_GROUPS + g) * SSM_STATE:SSM_DIM + (SSM_GROUPS + g + 1) * SSM_STATE].astype(_BF)
        gmat = _dot_nt(cm, bm)
        yo = hprev_fn(g, cm)
        for hh in range(heads_per_group):
            h = g * heads_per_group + hh
            hs = slice(h * SSM_HEAD_DIM, (h + 1) * SSM_HEAD_DIM)
            seg = acum[:, h:h + 1] - acum_t[h:h + 1, :]
            lmat = jnp.exp(jnp.where(mask, seg, -jnp.inf))
            sc = gmat * (lmat * dt_t[h:h + 1, :])
            xh = xa_ref[:, hs]
            yd = _dot(sc.astype(_BF), xh.astype(_BF))
            yoh = yo[:, hh * SSM_HEAD_DIM:(hh + 1) * SSM_HEAD_DIM] * ea[:, h:h + 1]
            y_ref[:, hs] = yd + yoh + xh * d_ref[:, hs]
            xw_ref[:, hs] = xh * w[:, h:h + 1]
    return acum, atot


def _ssd_gate_norm(y, z, ng_ref, o_ref):
    gated = y * _silu(z)
    gw = SSM_DIM // SSM_GROUPS
    for g in range(SSM_GROUPS):
        cs = slice(g * gw, (g + 1) * gw)
        gg = gated[:, cs]
        ms = jnp.mean(gg * gg, axis=-1, keepdims=True)
        o_ref[:, cs] = (gg * lax.rsqrt(ms + EPS) * ng_ref[:, cs]).astype(o_ref.dtype)


def _short_conv_silu(xp_ref, cw_ref, cb_ref, xa_ref, lead):
    for c in range(XBC_DIM // LANE):
        cs = slice(c * LANE, (c + 1) * LANE)
        acc = jnp.zeros((CHUNK, LANE), _F32) + cb_ref[:, cs]
        for k in range(SSM_CONV):
            off = lead - (SSM_CONV - 1) + k
            acc = acc + cw_ref[k:k + 1, cs] * xp_ref[off:off + CHUNK, cs]
        xa_ref[:, cs] = _silu(acc)


def _ssd_prompt_kernel(xbc_ref, z_ref, dt_ref, cw_ref, cb_ref, dtb_ref, a_ref, d_ref, ng_ref,
                       o_ref, hout_ref, xp_ref, xa_ref, y_ref, xw_ref, h_ref, *, l_valid):
    c = pl.program_id(1)

    @pl.when(c == 0)
    def _():
        xp_ref[0:SUBLANE, :] = jnp.zeros((SUBLANE, XBC_DIM), _F32)
        h_ref[...] = jnp.zeros(h_ref.shape, _F32)

    xp_ref[SUBLANE:SUBLANE + CHUNK, :] = xbc_ref[...]
    _short_conv_silu(xp_ref, cw_ref, cb_ref, xa_ref, SUBLANE)
    xp_ref[0:SUBLANE, :] = xp_ref[CHUNK:CHUNK + SUBLANE, :]

    ri = lax.broadcasted_iota(jnp.int32, (CHUNK, CHUNK), 0)
    ci = lax.broadcasted_iota(jnp.int32, (CHUNK, CHUNK), 1)
    mask = ci <= ri
    tmat = jnp.where(mask, 1.0, 0.0).astype(_BF)
    pos = c * CHUNK + lax.broadcasted_iota(jnp.int32, (CHUNK, LANE), 0)
    dt = jnp.where(pos < l_valid, _softplus(dt_ref[...] + dtb_ref[...]), 0.0)

    hb = h_ref[...].astype(_BF)
    gw = SSM_DIM // SSM_GROUPS

    def hprev_fn(g, cm):
        return _dot_nt(cm, hb[g * gw:(g + 1) * gw, :])

    acum, atot = _ssd_chunk(xa_ref, dt, a_ref[...], mask, tmat, hprev_fn, d_ref, y_ref, xw_ref)
    _ssd_gate_norm(y_ref[...], z_ref[...], ng_ref, o_ref)

    xw_t = xw_ref[...].T
    dec = jnp.exp(atot)
    for g in range(SSM_GROUPS):
        bm = xa_ref[:, SSM_DIM + g * SSM_STATE:SSM_DIM + (g + 1) * SSM_STATE].astype(_BF)
        st = _dot(xw_t[g * gw:(g + 1) * gw, :].astype(_BF), bm)
        for hh in range(SSM_HEADS // SSM_GROUPS):
            h = g * (SSM_HEADS // SSM_GROUPS) + hh
            hs = slice(h * SSM_HEAD_DIM, (h + 1) * SSM_HEAD_DIM)
            h_ref[hs, :] = h_ref[hs, :] * dec[:, h:h + 1] + st[hh * SSM_HEAD_DIM:(hh + 1) * SSM_HEAD_DIM, :]

    @pl.when(c == pl.num_programs(1) - 1)
    def _():
        hout_ref[...] = h_ref[...]


def _ssd_prompt(u, lw, nb, lp, l_valid):
    t = u.shape[0]
    nc = lp // CHUNK
    row = lambda b, i: b * nc + i
    full = lambda shape: pl.BlockSpec(shape, lambda b, i: (0,) * len(shape))
    return pl.pallas_call(
        functools.partial(_ssd_prompt_kernel, l_valid=l_valid),
        grid=(nb, nc),
        in_specs=[
            pl.BlockSpec((CHUNK, XBC_DIM), lambda b, i: (row(b, i), U_XBC // XBC_DIM)),
            pl.BlockSpec((CHUNK, SSM_DIM), lambda b, i: (row(b, i), U_Z // SSM_DIM)),
            pl.BlockSpec((CHUNK, LANE), lambda b, i: (row(b, i), U_DT // LANE)),
            full((SSM_CONV, XBC_DIM)), full((1, XBC_DIM)), full((1, LANE)), full((1, LANE)),
            full((1, SSM_DIM)), full((1, SSM_DIM)),
        ],
        out_specs=[
            pl.BlockSpec((CHUNK, SSM_DIM), lambda b, i: (row(b, i), 0)),
            pl.BlockSpec((None, SSM_DIM, SSM_STATE), lambda b, i: (b, 0, 0)),
        ],
        out_shape=[jax.ShapeDtypeStruct((t, SSM_DIM), _BF), jax.ShapeDtypeStruct((nb, SSM_DIM, SSM_STATE), _F32)],
        scratch_shapes=[
            pltpu.VMEM((CHUNK + SUBLANE, XBC_DIM), _F32), pltpu.VMEM((CHUNK, XBC_DIM), _F32),
            pltpu.VMEM((CHUNK, SSM_DIM), _F32), pltpu.VMEM((CHUNK, SSM_DIM), _F32),
            pltpu.VMEM((SSM_DIM, SSM_STATE), _F32),
        ],
        compiler_params=_cparams(("parallel", "arbitrary")),
        name="ssd_prompt",
    )(u, u, u, lw["scw"], lw["scb"], lw["dtb"], lw["a"], lw["dskip"], lw["sng"])


def _ssd_sample_kernel(xbc_ref, z_ref, dt_ref, left_ref, h0_ref, cw_ref, cb_ref, dtb_ref, a_ref, d_ref, ng_ref,
                       o_ref, hout_ref, xp_ref, xa_ref, y_ref, xw_ref, yo_ref):
    xp_ref[:, 0:SUBLANE, :] = left_ref[...]
    xp_ref[:, SUBLANE:2 * SUBLANE, :] = xbc_ref[...].reshape(REQ_BLK, SUBLANE, XBC_DIM)
    for c in range(XBC_DIM // LANE):
        cs = slice(c * LANE, (c + 1) * LANE)
        acc = jnp.zeros((REQ_BLK, SUBLANE, LANE), _F32) + cb_ref[:, cs][None]
        for k in range(SSM_CONV):
            off = SUBLANE - (SSM_CONV - 1) + k
            acc = acc + cw_ref[k:k + 1, cs][None] * xp_ref[:, off:off + SUBLANE, cs]
        xa_ref[:, cs] = _silu(acc).reshape(CHUNK, LANE)

    ri = lax.broadcasted_iota(jnp.int32, (CHUNK, CHUNK), 0)
    ci = lax.broadcasted_iota(jnp.int32, (CHUNK, CHUNK), 1)
    same = (ri >> 3) == (ci >> 3)
    mask = jnp.logical_and(same, ci <= ri)
    tmat = jnp.where(mask, 1.0, 0.0).astype(_BF)
    lastmat = jnp.where(same, 1.0, 0.0).astype(_BF)
    dt = _softplus(dt_ref[...] + dtb_ref[...])
    gw = SSM_DIM // SSM_GROUPS

    for r in range(REQ_BLK):
        rs = slice(r * SUBLANE, (r + 1) * SUBLANE)
        for g in range(SSM_GROUPS):
            cm = xa_ref[rs, SSM_DIM + (SSM_GROUPS + g) * SSM_STATE:SSM_DIM + (SSM_GROUPS + g + 1) * SSM_STATE]
            hb = h0_ref[r, g * gw:(g + 1) * gw, :].astype(_BF)
            yo_ref[rs, g * gw:(g + 1) * gw] = _dot_nt(cm.astype(_BF), hb)

    def hprev_fn(g, cm):
        return yo_ref[:, g * gw:(g + 1) * gw]

    acum, atot = _ssd_chunk(xa_ref, dt, a_ref[...], mask, tmat, hprev_fn, d_ref, y_ref, xw_ref, lastmat=lastmat)
    _ssd_gate_norm(y_ref[...], z_ref[...], ng_ref, o_ref)

    xw_t = xw_ref[...].T
    dec = jnp.exp(atot)
    lane_req = lax.broadcasted_iota(jnp.int32, (gw, CHUNK), 1) >> 3
    for g in range(SSM_GROUPS):
        bm = xa_ref[:, SSM_DIM + g * SSM_STATE:SSM_DIM + (g + 1) * SSM_STATE].astype(_BF)
        xg = xw_t[g * gw:(g + 1) * gw, :]
        for r in range(REQ_BLK):
            st = _dot(jnp.where(lane_req == r, xg, 0.0).astype(_BF), bm)
            for hh in range(SSM_HEADS // SSM_GROUPS):
                h = g * (SSM_HEADS // SSM_GROUPS) + hh
                hs = slice(h * SSM_HEAD_DIM, (h + 1) * SSM_HEAD_DIM)
                d = dec[r * SUBLANE:r * SUBLANE + 1, h:h + 1]
                hout_ref[r, hs, :] = h0_ref[r, hs, :] * d + st[hh * SSM_HEAD_DIM:(hh + 1) * SSM_HEAD_DIM, :]


def _ssd_sample(u, left8, h0_all, layer, lw):
    t = u.shape[0]
    nreq = t // SUBLANE
    full = lambda shape: pl.BlockSpec(shape, lambda i: (0,) * len(shape))
    return pl.pallas_call(
        _ssd_sample_kernel,
        grid=(nreq // REQ_BLK,),
        in_specs=[
            pl.BlockSpec((CHUNK, XBC_DIM), lambda i: (i, U_XBC // XBC_DIM)),
            pl.BlockSpec((CHUNK, SSM_DIM), lambda i: (i, U_Z // SSM_DIM)),
            pl.BlockSpec((CHUNK, LANE), lambda i: (i, U_DT // LANE)),
            pl.BlockSpec((REQ_BLK, SUBLANE, XBC_DIM), lambda i: (i, 0, 0)),
            pl.BlockSpec((None, REQ_BLK, SSM_DIM, SSM_STATE), lambda i: (layer, i, 0, 0)),
            full((SSM_CONV, XBC_DIM)), full((1, XBC_DIM)), full((1, LANE)), full((1, LANE)),
            full((1, SSM_DIM)), full((1, SSM_DIM)),
        ],
        out_specs=[
            pl.BlockSpec((CHUNK, SSM_DIM), lambda i: (i, 0)),
            pl.BlockSpec((REQ_BLK, SSM_DIM, SSM_STATE), lambda i: (i, 0, 0)),
        ],
        out_shape=[jax.ShapeDtypeStruct((t, SSM_DIM), _BF), jax.ShapeDtypeStruct((nreq, SSM_DIM, SSM_STATE), _F32)],
        scratch_shapes=[
            pltpu.VMEM((REQ_BLK, 2 * SUBLANE, XBC_DIM), _F32), pltpu.VMEM((CHUNK, XBC_DIM), _F32),
            pltpu.VMEM((CHUNK, SSM_DIM), _F32), pltpu.VMEM((CHUNK, SSM_DIM), _F32),
            pltpu.VMEM((CHUNK, SSM_DIM), _F32),
        ],
        compiler_params=_cparams(("parallel",)),
        name="ssd_sample",
    )(u, u, u, left8, h0_all, lw["scw"], lw["scb"], lw["dtb"], lw["a"], lw["dskip"], lw["sng"])


def _outproj_kernel(cy_ref, my_ref, sy_ref, h_ref, w_ref, o_ref, mix_ref):
    @pl.when(pl.program_id(1) == 0)
    def _():
        mix_ref[:, 0:CONV_DIM] = cy_ref[...].astype(_BF)
        mix_ref[:, CONV_DIM:CONV_DIM + MLA_DIM] = my_ref[...].astype(_BF)
        mix_ref[:, CONV_DIM + MLA_DIM:] = sy_ref[...].astype(_BF)

    o_ref[...] = h_ref[...] + _dot(mix_ref[...], w_ref[...])


def _outproj(cy, my, sy, h, w_all, layer, tm, tn=D_MODEL):
    t = h.shape[0]
    return pl.pallas_call(
        _outproj_kernel,
        grid=(t // tm, D_MODEL // tn),
        in_specs=[
            pl.BlockSpec((tm, CONV_DIM), lambda i, j: (i, 0)),
            pl.BlockSpec((tm, MLA_DIM), lambda i, j: (i, 0)),
            pl.BlockSpec((tm, SSM_DIM), lambda i, j: (i, 0)),
            pl.BlockSpec((tm, tn), lambda i, j: (i, j)),
            pl.BlockSpec((None, D_MODEL, tn), lambda i, j: (layer, 0, j)),
        ],
        out_specs=pl.BlockSpec((tm, tn), lambda i, j: (i, j)),
        out_shape=jax.ShapeDtypeStruct((t, D_MODEL), _F32),
        scratch_shapes=[pltpu.VMEM((tm, D_MODEL), _BF)],
        compiler_params=_cparams(("parallel", "arbitrary")),
        name="outproj",
    )(cy, my, sy, h, w_all)


def _final_norm_kernel(x_ref, g_ref, o_ref):
    x = x_ref[...]
    ms = jnp.mean(x * x, axis=-1, keepdims=True)
    o_ref[...] = x * lax.rsqrt(ms + EPS) * g_ref[...]


def _final_norm(h, g, tm):
    t = h.shape[0]
    return pl.pallas_call(
        _final_norm_kernel,
        grid=(t // tm,),
        in_specs=[pl.BlockSpec((tm, D_MODEL), lambda i: (i, 0)), pl.BlockSpec((1, D_MODEL), lambda i: (0, 0))],
        out_specs=pl.BlockSpec((tm, D_MODEL), lambda i: (i, 0)),
        out_shape=jax.ShapeDtypeStruct((t, D_MODEL), _F32),
        compiler_params=_cparams(("parallel",)),
        name="final_norm",
    )(h, g)


def _rot_half_cols(w):
    half = ROPE_DIM // 2
    return jnp.concatenate([-w[..., half:], w[..., :half]], axis=-1)


def _prep_weights(norm_g, w_in, conv_dw_w, conv_dw_b, conv_ln_g, conv_ln_b, conv_pw_w, conv_pw_b, mla_qa_g,
                  mla_wq_b, mla_kva_g, mla_wkv_b, ssm_conv_w, ssm_conv_b, ssm_dt_bias, ssm_A_log, ssm_D,
                  ssm_norm_g, w_out):
    w_in_t = jnp.swapaxes(w_in, 1, 2)
    o = 0
    parts = {}
    for name, width in (("glu", 2 * CONV_DIM), ("cg", CONV_DIM), ("cq", Q_LORA), ("ckv", KV_LORA),
                        ("kpe", ROPE_DIM), ("mg", MLA_DIM), ("z", SSM_DIM), ("xbc", XBC_DIM), ("dt", SSM_HEADS)):
        parts[name] = w_in_t[:, o:o + width, :]
        o += width
    half = ROPE_DIM // 2
    kpe_rot = jnp.concatenate([-parts["kpe"][:, half:], parts["kpe"][:, :half]], axis=1)
    dt_pad = jnp.pad(parts["dt"], ((0, 0), (0, LANE - SSM_HEADS), (0, 0)))
    w_in_p = jnp.concatenate(
        [parts["glu"], parts["mg"], parts["xbc"], parts["cg"], parts["cq"], parts["z"], parts["ckv"],
         parts["kpe"], kpe_rot, dt_pad], axis=1).astype(_BF)

    depth = w_in.shape[0]
    wq_nope = mla_wq_b[..., :NOPE_DIM].reshape(depth, Q_LORA, MLA_HEADS * NOPE_DIM)
    wq_rope = mla_wq_b[..., NOPE_DIM:]
    wq_slots = jnp.concatenate([wq_rope, _rot_half_cols(wq_rope)], axis=-1).reshape(depth, Q_LORA, MLA_HEADS * LANE)
    wq = jnp.concatenate([wq_nope, wq_slots], axis=-1).astype(_BF)
    wuk = jnp.transpose(mla_wkv_b[..., :NOPE_DIM], (0, 2, 3, 1)).astype(_BF)
    wqt = jnp.swapaxes(wq, 1, 2)
    wukt = jnp.transpose(mla_wkv_b[..., :NOPE_DIM], (0, 2, 1, 3)).astype(_BF)
    wuvt = jnp.transpose(mla_wkv_b[..., NOPE_DIM:], (0, 2, 3, 1)).astype(_BF)
    wuv_all = mla_wkv_b[..., NOPE_DIM:].reshape(depth, KV_LORA, MLA_DIM).astype(_BF)

    def row(x):
        return x[:, None, :].astype(_F32)

    pad_heads = lambda x: jnp.pad(x, ((0, 0), (0, LANE - SSM_HEADS)))
    layers = []
    for l in range(depth):
        layers.append({
            "norm_g": row(norm_g)[l],
            "dww": conv_dw_w[l], "dwb": row(conv_dw_b)[l], "lng": row(conv_ln_g)[l], "lnb": row(conv_ln_b)[l],
            "pww": conv_pw_w[l].astype(_BF), "pwb": row(conv_pw_b)[l],
            "qag": row(mla_qa_g)[l], "wq": wq[l], "wuk": wuk[l], "kvg": row(mla_kva_g)[l],
            "wqt": wqt[l], "wukt": wukt[l], "wuvt": wuvt[l], "wuv_all": wuv_all[l],
            "scw": ssm_conv_w[l], "scb": row(ssm_conv_b)[l],
            "dtb": row(pad_heads(ssm_dt_bias))[l], "a": row(pad_heads(-jnp.exp(ssm_A_log.astype(_F32))))[l],
            "dskip": row(jnp.repeat(ssm_D, SSM_HEAD_DIM, axis=-1))[l], "sng": row(ssm_norm_g)[l],
        })
    return layers, w_in_p, w_out.astype(_BF)


def _rope_table(pos):
    inv = ROPE_THETA ** (-jnp.arange(0, ROPE_DIM, 2, dtype=_F32) / ROPE_DIM)
    ang = pos.astype(_F32)[:, None] * inv[None, :]
    c, s = jnp.cos(ang), jnp.sin(ang)
    return jnp.concatenate([c, c, s, s], axis=-1)


def _largest_tile(t, cands):
    for c in cands:
        if t % c == 0:
            return c
    raise ValueError(f"no row tile for {t}")


def kernel(x_prompt, x_sample, cache_ckv, cache_kpe, page_table, state_conv, state_ssm_conv, state_ssm, meta_tokens, norm_g, w_in, conv_dw_w, conv_dw_b, conv_ln_g, conv_ln_b, conv_pw_w, conv_pw_b, mla_qa_g, mla_wq_b, mla_kva_g, mla_wkv_b, ssm_conv_w, ssm_conv_b, ssm_dt_bias, ssm_A_log, ssm_D, ssm_norm_g, w_out, final_g):
    nb, seq, _ = x_prompt.shape
    nreq, dseq, _ = x_sample.shape
    depth = w_in.shape[0]
    n_pages = page_table.shape[1]
    page_size = cache_ckv.shape[2]
    assert dseq == SUBLANE and nreq % REQ_BLK == 0 and page_size == LANE
    l_valid = N_META + seq
    lp = -(-l_valid // ATT_BLK) * ATT_BLK
    tp, ts = nb * lp, nreq * dseq

    layers, w_in_all, w_out_all = _prep_weights(
        norm_g, w_in, conv_dw_w, conv_dw_b, conv_ln_g, conv_ln_b, conv_pw_w, conv_pw_b, mla_qa_g, mla_wq_b,
        mla_kva_g, mla_wkv_b, ssm_conv_w, ssm_conv_b, ssm_dt_bias, ssm_A_log, ssm_D, ssm_norm_g, w_out)

    meta = jnp.broadcast_to(meta_tokens[None].astype(_F32), (nb, N_META, D_MODEL))
    hp = jnp.concatenate([meta, x_prompt, jnp.zeros((nb, lp - l_valid, D_MODEL), _F32)], axis=1).reshape(tp, D_MODEL)
    hs = x_sample.reshape(ts, D_MODEL)
    cs_p = jnp.tile(_rope_table(jnp.arange(lp, dtype=jnp.int32)), (nb, 1))
    cst_p = cs_p.T
    past_len = n_pages * page_size
    cs_s = jnp.tile(_rope_table(past_len + jnp.arange(dseq, dtype=jnp.int32)), (nreq, 1))
    pt_flat = page_table.reshape(-1).astype(jnp.int32)
    cache_kpe_t = jnp.swapaxes(cache_kpe, 2, 3)
    left_conv = jnp.pad(state_conv, ((0, 0), (0, 0), (2, 0), (0, 0)))
    left_ssm = jnp.pad(state_ssm_conv, ((0, 0), (0, 0), (SUBLANE - (SSM_CONV - 1), 0), (0, 0)))
    h0_all = state_ssm.reshape(depth, nreq, SSM_DIM, SSM_STATE)

    tm_p = _largest_tile(tp, (768, 512, 384, 256, 128))
    tm_s = _largest_tile(ts, (512, 256, 128))
    tq_p = _largest_tile(tp, (384, 256, 128))
    tq_s = _largest_tile(ts, (256, 128))

    st_p = [[] for _ in range(5)]
    st_s = [[] for _ in range(5)]
    for l in range(depth):
        lw = layers[l]
        u = _inproj(hp, lw["norm_g"], w_in_all, l, tm_p)
        v, cy = _conv_prompt(u, lw, nb, lp)
        qt, kcat, vt, ckv, kpe = _qkt(u, cs_p, cst_p, lw, nb, lp)
        my = _attn_prompt(qt, kcat, vt, u, lw["wuvt"], nb, lp)
        sy, hfin = _ssd_prompt(u, lw, nb, lp, l_valid)
        hp = _outproj(cy, my, sy, hp, w_out_all, l, tm_p)
        u3 = u.reshape(nb, lp, U_TOTAL)
        st_p[0].append(ckv.reshape(nb, lp, KV_LORA)[:, :l_valid])
        st_p[1].append(kpe.reshape(nb, lp, ROPE_DIM)[:, :l_valid])
        st_p[2].append(v.reshape(nb, lp, CONV_DIM)[:, l_valid - (CONV_WIDTH - 1):l_valid])
        st_p[3].append(u3[:, l_valid - (SSM_CONV - 1):l_valid, U_XBC:U_XBC + XBC_DIM])
        st_p[4].append(hfin.reshape(nb, SSM_HEADS, SSM_HEAD_DIM, SSM_STATE))
        u = _inproj(hs, lw["norm_g"], w_in_all, l, tm_s)
        cst, cy = _conv_sample(u, left_conv[l], lw)
        q, kcat, ckv, kpe = _qk(u, cs_s, lw, tq_s, _F32)
        my = _attn_sample(q, kcat, u, lw["wuv_all"], cache_ckv, cache_kpe_t, pt_flat, l, n_pages)
        sy, hnew = _ssd_sample(u, left_ssm[l], h0_all, l, lw)
        hs = _outproj(cy, my, sy, hs, w_out_all, l, tm_s)
        u3 = u.reshape(nreq, dseq, U_TOTAL)
        st_s[0].append(ckv.reshape(nreq, dseq, KV_LORA))
        st_s[1].append(kpe.reshape(nreq, dseq, ROPE_DIM))
        st_s[2].append(cst[:, 2:, :])
        st_s[3].append(u3[:, dseq - (SSM_CONV - 1):, U_XBC:U_XBC + XBC_DIM])
        st_s[4].append(hnew.reshape(nreq, SSM_HEADS, SSM_HEAD_DIM, SSM_STATE))

    y_prompt = _final_norm(hp, final_g[None, :].astype(_F32), tm_p).reshape(nb, lp, D_MODEL)[:, N_META:l_valid]
    y_sample = _final_norm(hs, final_g[None, :].astype(_F32), tm_s).reshape(nreq, dseq, D_MODEL)
    outs_p = [jnp.stack(s) for s in st_p]
    outs_s = [jnp.stack(s) for s in st_s]
    return (y_prompt, y_sample, *outs_p, *outs_s)
```

```python
import functools
import math

import jax
import jax.numpy as jnp
from jax import lax
from jax.experimental import pallas as pl
from jax.experimental.pallas import tpu as pltpu

D_MODEL = 2048
DEPTH = 4
N_META = 16
CONV_DIM = 512
CONV_WIDTH = 31
MLA_HEADS = 8
MLA_V_DIM = 128
MLA_DIM = MLA_HEADS * MLA_V_DIM
Q_LORA = 512
KV_LORA = 256
NOPE_DIM = 128
ROPE_DIM = 64
ROPE_THETA = 10000.0
MLA_SCALE = 1.0 / math.sqrt(NOPE_DIM + ROPE_DIM)
SSM_DIM = 512
SSM_HEAD_DIM = 64
SSM_HEADS = 8
SSM_GROUPS = 2
SSM_STATE = 128
SSM_CONV = 4
XBC_DIM = SSM_DIM + 2 * SSM_GROUPS * SSM_STATE
EPS = 1e-6

LANE = 128
SUBLANE = 8
VMEM_LIMIT = 48 * 1024 * 1024

U_GLU = 0
U_MG = 1024
U_XBC = 2048
U_CG = 3072
U_CQ = 3584
U_Z = 4096
U_CKV = 4608
U_KPE = 4864
U_DT = 4992
U_TOTAL = 5120

QK_PAD = 384
CHUNK = 128
ATT_BLK = 256
EXP2_SCALE = MLA_SCALE * math.log2(math.e)
_BF = jnp.bfloat16
_F32 = jnp.float32


def _cparams(sem):
    return pltpu.CompilerParams(dimension_semantics=sem, vmem_limit_bytes=VMEM_LIMIT)


def _dot(a, b):
    return jnp.dot(a, b, preferred_element_type=_F32)


def _dot_nt(a, b):
    return lax.dot_general(a, b, (((1,), (1,)), ((), ())), preferred_element_type=_F32)


def _sigmoid(x):
    return 1.0 / (1.0 + jnp.exp(-x))


def _silu(x):
    return x * _sigmoid(x)


def _softplus(x):
    return jnp.maximum(x, 0.0) + jnp.log(1.0 + jnp.exp(-jnp.abs(x)))


def _exact_dot_01(mat01_bf16, x):
    hi = x.astype(_BF)
    r1 = x - hi.astype(_F32)
    mid = r1.astype(_BF)
    lo = (r1 - mid.astype(_F32)).astype(_BF)
    return _dot(mat01_bf16, hi) + _dot(mat01_bf16, mid) + _dot(mat01_bf16, lo)


def _inproj_kernel(x_ref, g_ref, w_ref, o_ref, xn_ref):
    @pl.when(pl.program_id(1) == 0)
    def _():
        x = x_ref[...]
        ms = jnp.mean(x * x, axis=-1, keepdims=True)
        xn_ref[...] = (x * lax.rsqrt(ms + EPS) * g_ref[...]).astype(_BF)

    o_ref[...] = _dot_nt(xn_ref[...], w_ref[...])


def _inproj(h, g, w_all, layer, tm, tn=U_TOTAL // 2):
    t = h.shape[0]
    n = w_all.shape[1]
    return pl.pallas_call(
        _inproj_kernel,
        grid=(t // tm, n // tn),
        in_specs=[
            pl.BlockSpec((tm, D_MODEL), lambda i, j: (i, 0)),
            pl.BlockSpec((1, D_MODEL), lambda i, j: (0, 0)),
            pl.BlockSpec((None, tn, D_MODEL), lambda i, j: (layer, j, 0)),
        ],
        out_specs=pl.BlockSpec((tm, tn), lambda i, j: (i, j)),
        out_shape=jax.ShapeDtypeStruct((t, n), _F32),
        scratch_shapes=[pltpu.VMEM((tm, D_MODEL), _BF)],
        compiler_params=_cparams(("parallel", "arbitrary")),
        name="inproj",
    )(h, g, w_all)


def _conv_tail(y, cg, lng_ref, lnb_ref, pww_ref, pwb_ref):
    mu = jnp.mean(y, axis=-1, keepdims=True)
    yc = y - mu
    var = jnp.mean(yc * yc, axis=-1, keepdims=True)
    yn = yc * lax.rsqrt(var + EPS) * lng_ref[...] + lnb_ref[...]
    s = _silu(yn)
    o = _dot(s.astype(_BF), pww_ref[...]) + pwb_ref[...]
    return o * _silu(cg)


def _conv_prompt_kernel(glu_ref, cg_ref, dww_ref, dwb_ref, lng_ref, lnb_ref, pww_ref, pwb_ref,
                        v_ref, cy_ref, vp_ref, y_ref, sh_ref):
    @pl.when(pl.program_id(1) == 0)
    def _():
        vp_ref[0:32, :] = jnp.zeros((32, CONV_DIM), _F32)

    v = glu_ref[:, :CONV_DIM] * _sigmoid(glu_ref[:, CONV_DIM:])
    v_ref[...] = v
    vp_ref[32:32 + CHUNK, :] = v
    for c in range(CONV_DIM // LANE):
        cs = slice(c * LANE, (c + 1) * LANE)
        acc = jnp.zeros((CHUNK, LANE), _F32)
        for r in range(SUBLANE):
            taps = range(r, CONV_WIDTH, SUBLANE)
            span = CHUNK + SUBLANE * (len(taps) - 1)
            sh_ref[0:span, :] = vp_ref[2 + r:2 + r + span, cs]
            for j, k in enumerate(taps):
                acc = acc + dww_ref[k:k + 1, cs] * sh_ref[SUBLANE * j:SUBLANE * j + CHUNK, :]
        y_ref[:, cs] = acc + dwb_ref[:, cs]
    cy = _conv_tail(y_ref[...], cg_ref[...], lng_ref, lnb_ref, pww_ref, pwb_ref)
    cy_ref[...] = cy.astype(cy_ref.dtype)
    vp_ref[0:32, :] = vp_ref[CHUNK:CHUNK + 32, :]


def _conv_prompt(u, lw, nb, lp):
    t = u.shape[0]
    nt = lp // CHUNK
    row = lambda b, i: b * nt + i
    full = lambda shape: pl.BlockSpec(shape, lambda b, i: (0,) * len(shape))
    return pl.pallas_call(
        _conv_prompt_kernel,
        grid=(nb, nt),
        in_specs=[
            pl.BlockSpec((CHUNK, 2 * CONV_DIM), lambda b, i: (row(b, i), U_GLU // (2 * CONV_DIM))),
            pl.BlockSpec((CHUNK, CONV_DIM), lambda b, i: (row(b, i), U_CG // CONV_DIM)),
            full((CONV_WIDTH, CONV_DIM)), full((1, CONV_DIM)), full((1, CONV_DIM)), full((1, CONV_DIM)),
            full((CONV_DIM, CONV_DIM)), full((1, CONV_DIM)),
        ],
        out_specs=[
            pl.BlockSpec((CHUNK, CONV_DIM), lambda b, i: (row(b, i), 0)),
            pl.BlockSpec((CHUNK, CONV_DIM), lambda b, i: (row(b, i), 0)),
        ],
        out_shape=[jax.ShapeDtypeStruct((t, CONV_DIM), _F32), jax.ShapeDtypeStruct((t, CONV_DIM), _BF)],
        scratch_shapes=[pltpu.VMEM((CHUNK + 32, CONV_DIM), _F32), pltpu.VMEM((CHUNK, CONV_DIM), _F32),
                        pltpu.VMEM((CHUNK + 32, LANE), _F32)],
        compiler_params=_cparams(("parallel", "arbitrary")),
        name="conv_prompt",
    )(u, u, lw["dww"], lw["dwb"], lw["lng"], lw["lnb"], lw["pww"], lw["pwb"])


REQ_BLK = CHUNK // SUBLANE


def _conv_sample_kernel(glu_ref, cg_ref, left_ref, dww_ref, dwb_ref, lng_ref, lnb_ref, pww_ref, pwb_ref,
                        st_ref, cy_ref, vp_ref, y_ref):
    v = glu_ref[:, :CONV_DIM] * _sigmoid(glu_ref[:, CONV_DIM:])
    vp_ref[:, 0:32, :] = left_ref[...]
    vp_ref[:, 32:40, :] = v.reshape(REQ_BLK, SUBLANE, CONV_DIM)
    for c in range(CONV_DIM // LANE):
        cs = slice(c * LANE, (c + 1) * LANE)
        acc = jnp.zeros((REQ_BLK, SUBLANE, LANE), _F32)
        for k in range(CONV_WIDTH):
            acc = acc + dww_ref[k:k + 1, cs][None] * vp_ref[:, 2 + k:2 + k + SUBLANE, cs]
        y_ref[:, :, cs] = acc + dwb_ref[:, cs][None]
    y = y_ref[...].reshape(CHUNK, CONV_DIM)
    cy = _conv_tail(y, cg_ref[...], lng_ref, lnb_ref, pww_ref, pwb_ref)
    cy_ref[...] = cy.astype(cy_ref.dtype)
    st_ref[...] = vp_ref[:, 8:40, :]


def _conv_sample(u, left32, lw):
    t = u.shape[0]
    nreq = t // SUBLANE
    full = lambda shape: pl.BlockSpec(shape, lambda i: (0,) * len(shape))
    return pl.pallas_call(
        _conv_sample_kernel,
        grid=(nreq // REQ_BLK,),
        in_specs=[
            pl.BlockSpec((CHUNK, 2 * CONV_DIM), lambda i: (i, U_GLU // (2 * CONV_DIM))),
            pl.BlockSpec((CHUNK, CONV_DIM), lambda i: (i, U_CG // CONV_DIM)),
            pl.BlockSpec((REQ_BLK, 32, CONV_DIM), lambda i: (i, 0, 0)),
            full((CONV_WIDTH, CONV_DIM)), full((1, CONV_DIM)), full((1, CONV_DIM)), full((1, CONV_DIM)),
            full((CONV_DIM, CONV_DIM)), full((1, CONV_DIM)),
        ],
        out_specs=[
            pl.BlockSpec((REQ_BLK, 32, CONV_DIM), lambda i: (i, 0, 0)),
            pl.BlockSpec((CHUNK, CONV_DIM), lambda i: (i, 0)),
        ],
        out_shape=[jax.ShapeDtypeStruct((nreq, 32, CONV_DIM), _F32), jax.ShapeDtypeStruct((t, CONV_DIM), _BF)],
        scratch_shapes=[pltpu.VMEM((REQ_BLK, 40, CONV_DIM), _F32), pltpu.VMEM((REQ_BLK, SUBLANE, CONV_DIM), _F32)],
        compiler_params=_cparams(("parallel",)),
        name="conv_sample",
    )(u, u, left32, lw["dww"], lw["dwb"], lw["lng"], lw["lnb"], lw["pww"], lw["pwb"])


def _rope_slot(slot, cs):
    r = slot * cs
    r = r + pltpu.roll(r, 64, axis=1)
    lane = lax.broadcasted_iota(jnp.int32, r.shape, 1)
    return jnp.where(lane < ROPE_DIM, r, 0.0)


def _qk_kernel(cq_ref, ckv_ref, kpe_ref, cs_ref, qag_ref, wq_ref, wuk_ref, kvg_ref,
               q_ref, kcat_ref, ckvo_ref, kpeo_ref):
    cq = cq_ref[...]
    ms = jnp.mean(cq * cq, axis=-1, keepdims=True)
    cqn = (cq * lax.rsqrt(ms + EPS) * qag_ref[...]).astype(_BF)
    q = _dot(cqn, wq_ref[...])
    cs = cs_ref[...]
    for h in range(MLA_HEADS):
        qn = q[:, h * NOPE_DIM:(h + 1) * NOPE_DIM].astype(_BF)
        ql = _dot(qn, wuk_ref[h])
        q_ref[h, :, 0:KV_LORA] = ql.astype(q_ref.dtype)
        slot = q[:, MLA_HEADS * NOPE_DIM + h * LANE:MLA_HEADS * NOPE_DIM + (h + 1) * LANE]
        q_ref[h, :, KV_LORA:QK_PAD] = _rope_slot(slot, cs).astype(q_ref.dtype)
    c = ckv_ref[...]
    ms = jnp.mean(c * c, axis=-1, keepdims=True)
    ckv = c * lax.rsqrt(ms + EPS) * kvg_ref[...]
    ckvo_ref[...] = ckv
    kcat_ref[:, 0:KV_LORA] = ckv.astype(kcat_ref.dtype)
    kr = _rope_slot(kpe_ref[...], cs)
    kcat_ref[:, KV_LORA:QK_PAD] = kr.astype(kcat_ref.dtype)
    kpeo_ref[...] = kr[:, :ROPE_DIM]


def _qk(u, cs, lw, tm, qdtype):
    t = u.shape[0]
    full = lambda shape: pl.BlockSpec(shape, lambda i: (0,) * len(shape))
    return pl.pallas_call(
        _qk_kernel,
        grid=(t // tm,),
        in_specs=[
            pl.BlockSpec((tm, Q_LORA), lambda i: (i, U_CQ // Q_LORA)),
            pl.BlockSpec((tm, KV_LORA), lambda i: (i, U_CKV // KV_LORA)),
            pl.BlockSpec((tm, LANE), lambda i: (i, U_KPE // LANE)),
            pl.BlockSpec((tm, LANE), lambda i: (i, 0)),
            full((1, Q_LORA)), full((Q_LORA, 2 * MLA_HEADS * NOPE_DIM)),
            full((MLA_HEADS, NOPE_DIM, KV_LORA)), full((1, KV_LORA)),
        ],
        out_specs=[
            pl.BlockSpec((MLA_HEADS, tm, QK_PAD), lambda i: (0, i, 0)),
            pl.BlockSpec((tm, QK_PAD), lambda i: (i, 0)),
            pl.BlockSpec((tm, KV_LORA), lambda i: (i, 0)),
            pl.BlockSpec((tm, ROPE_DIM), lambda i: (i, 0)),
        ],
        out_shape=[
            jax.ShapeDtypeStruct((MLA_HEADS, t, QK_PAD), qdtype),
            jax.ShapeDtypeStruct((t, QK_PAD), qdtype),
            jax.ShapeDtypeStruct((t, KV_LORA), _F32),
            jax.ShapeDtypeStruct((t, ROPE_DIM), _F32),
        ],
        compiler_params=_cparams(("parallel",)),
        name="qk",
    )(u, u, u, cs, lw["qag"], lw["wq"], lw["wuk"], lw["kvg"])


def _qkt_kernel(cq_ref, ckv_ref, kpe_ref, cs_ref, cst_ref, qag_ref, wqt_ref, wukt_ref, kvg_ref,
                qt_ref, kcat_ref, vt_ref, ckvo_ref, kpeo_ref):
    tm = ATT_BLK
    cq = cq_ref[...]
    ms = jnp.mean(cq * cq, axis=-1, keepdims=True)
    cqn_t = (cq * lax.rsqrt(ms + EPS) * qag_ref[...]).T.astype(_BF)
    qt = _dot(wqt_ref[...], cqn_t)
    cst = cst_ref[...]
    for h in range(MLA_HEADS):
        cols = slice(h * tm, (h + 1) * tm)
        qn_t = qt[h * NOPE_DIM:(h + 1) * NOPE_DIM].astype(_BF)
        qt_ref[0:KV_LORA, cols] = _dot(wukt_ref[h], qn_t).astype(qt_ref.dtype)
        r = qt[MLA_HEADS * NOPE_DIM + h * LANE:MLA_HEADS * NOPE_DIM + (h + 1) * LANE] * cst
        qt_ref[KV_LORA:KV_LORA + ROPE_DIM, cols] = (r[:ROPE_DIM] + r[ROPE_DIM:]).astype(qt_ref.dtype)
        qt_ref[KV_LORA + ROPE_DIM:QK_PAD, cols] = jnp.zeros((QK_PAD - KV_LORA - ROPE_DIM, tm), qt_ref.dtype)
    c = ckv_ref[...]
    ms = jnp.mean(c * c, axis=-1, keepdims=True)
    ckv = c * lax.rsqrt(ms + EPS) * kvg_ref[...]
    ckvo_ref[...] = ckv
    kcat_ref[:, 0:KV_LORA] = ckv.astype(kcat_ref.dtype)
    vt_ref[...] = ckv.T.astype(vt_ref.dtype)
    kr = _rope_slot(kpe_ref[...], cs_ref[...])
    kcat_ref[:, KV_LORA:QK_PAD] = kr.astype(kcat_ref.dtype)
    kpeo_ref[...] = kr[:, :ROPE_DIM]


def _qkt(u, cs, cst, lw, nb, lp):
    t = u.shape[0]
    tm = ATT_BLK
    nblk = t // tm
    full = lambda shape: pl.BlockSpec(shape, lambda i: (0,) * len(shape))
    return pl.pallas_call(
        _qkt_kernel,
        grid=(nblk,),
        in_specs=[
            pl.BlockSpec((tm, Q_LORA), lambda i: (i, U_CQ // Q_LORA)),
            pl.BlockSpec((tm, KV_LORA), lambda i: (i, U_CKV // KV_LORA)),
            pl.BlockSpec((tm, LANE), lambda i: (i, U_KPE // LANE)),
            pl.BlockSpec((tm, LANE), lambda i: (i, 0)),
            pl.BlockSpec((LANE, tm), lambda i: (0, i)),
            full((1, Q_LORA)), full((2 * MLA_HEADS * NOPE_DIM, Q_LORA)),
            full((MLA_HEADS, KV_LORA, NOPE_DIM)), full((1, KV_LORA)),
        ],
        out_specs=[
            pl.BlockSpec((None, QK_PAD, MLA_HEADS * tm), lambda i: (i, 0, 0)),
            pl.BlockSpec((tm, QK_PAD), lambda i: (i, 0)),
            pl.BlockSpec((None, KV_LORA, tm), lambda i: (i, 0, 0)),
            pl.BlockSpec((tm, KV_LORA), lambda i: (i, 0)),
            pl.BlockSpec((tm, ROPE_DIM), lambda i: (i, 0)),
        ],
        out_shape=[
            jax.ShapeDtypeStruct((nblk, QK_PAD, MLA_HEADS * tm), _BF),
            jax.ShapeDtypeStruct((t, QK_PAD), _BF),
            jax.ShapeDtypeStruct((nblk, KV_LORA, tm), _BF),
            jax.ShapeDtypeStruct((t, KV_LORA), _F32),
            jax.ShapeDtypeStruct((t, ROPE_DIM), _F32),
        ],
        compiler_params=_cparams(("parallel",)),
        name="qkt",
    )(u, u, u, cs, cst, lw["qag"], lw["wqt"], lw["wukt"], lw["kvg"])


def _attn_prompt_kernel(qt_ref, k_ref, vt_ref, mg_ref, wuvt_ref, o_ref, m_ref, l_ref, acc_ref):
    qi = pl.program_id(1)
    ncol = MLA_HEADS * ATT_BLK
    kk = lax.broadcasted_iota(jnp.int32, (ATT_BLK, ATT_BLK), 0)
    qq = lax.broadcasted_iota(jnp.int32, (ATT_BLK, ATT_BLK), 1)
    diag_mask = jnp.concatenate([kk <= qq] * MLA_HEADS, axis=1)

    m_ref[...] = jnp.full((1, ncol), -jnp.inf, _F32)
    l_ref[...] = jnp.zeros((1, ncol), _F32)
    acc_ref[...] = jnp.zeros((KV_LORA, ncol), _F32)

    def chunk(c, masked):
        kc = k_ref[pl.ds(pl.multiple_of(c * ATT_BLK, ATT_BLK), ATT_BLK), :]
        s = _dot(kc, qt_ref[...])
        if masked:
            s = jnp.where(diag_mask, s, -jnp.inf)
        m_prev = m_ref[...]
        m_new = jnp.maximum(m_prev, jnp.max(s, axis=0, keepdims=True))
        alpha = jnp.exp2((m_prev - m_new) * EXP2_SCALE)
        p = jnp.exp2((s - m_new) * EXP2_SCALE)
        l_ref[...] = alpha * l_ref[...] + jnp.sum(p, axis=0, keepdims=True)
        acc_ref[...] = alpha * acc_ref[...] + _dot(vt_ref[c], p.astype(_BF))
        m_ref[...] = m_new

    def body(c, carry):
        chunk(c, False)
        return carry

    lax.fori_loop(0, qi, body, 0)
    chunk(qi, True)
    for h in range(MLA_HEADS):
        cols = slice(h * ATT_BLK, (h + 1) * ATT_BLK)
        o_t = (acc_ref[:, cols] / l_ref[:, cols]).astype(_BF)
        y = _dot(wuvt_ref[h], o_t).T
        g = mg_ref[:, h * MLA_V_DIM:(h + 1) * MLA_V_DIM]
        o_ref[:, h * MLA_V_DIM:(h + 1) * MLA_V_DIM] = (y * _silu(g)).astype(o_ref.dtype)


def _attn_prompt(qt, kcat, vt, u, wuvt, nb, lp):
    t = kcat.shape[0]
    nq = lp // ATT_BLK
    return pl.pallas_call(
        _attn_prompt_kernel,
        grid=(nb, nq),
        in_specs=[
            pl.BlockSpec((None, QK_PAD, MLA_HEADS * ATT_BLK), lambda b, i: (b * nq + i, 0, 0)),
            pl.BlockSpec((None, lp, QK_PAD), lambda b, i: (b, 0, 0)),
            pl.BlockSpec((None, nq, KV_LORA, ATT_BLK), lambda b, i: (b, 0, 0, 0)),
            pl.BlockSpec((ATT_BLK, MLA_DIM), lambda b, i: (b * nq + i, U_MG // MLA_DIM)),
            pl.BlockSpec((MLA_HEADS, MLA_V_DIM, KV_LORA), lambda b, i: (0, 0, 0)),
        ],
        out_specs=pl.BlockSpec((ATT_BLK, MLA_DIM), lambda b, i: (b * nq + i, 0)),
        out_shape=jax.ShapeDtypeStruct((t, MLA_DIM), _BF),
        scratch_shapes=[
            pltpu.VMEM((1, MLA_HEADS * ATT_BLK), _F32),
            pltpu.VMEM((1, MLA_HEADS * ATT_BLK), _F32),
            pltpu.VMEM((KV_LORA, MLA_HEADS * ATT_BLK), _F32),
        ],
        compiler_params=_cparams(("parallel", "arbitrary")),
        name="attn_prompt",
    )(qt, kcat.reshape(nb, lp, QK_PAD), vt.reshape(nb, nq, KV_LORA, ATT_BLK), u, wuvt)


def _page_copies(pt_ref, ckv_hbm, kpe_hbm, ckv_buf, kpe_buf, sem, req, slot, layer, n_pages):
    copies = []
    for i in range(n_pages):
        pg = pt_ref[req * n_pages + i]
        copies.append(pltpu.make_async_copy(
            ckv_hbm.at[layer, pg], ckv_buf.at[slot, pl.ds(i * LANE, LANE), :], sem.at[0, slot]))
        copies.append(pltpu.make_async_copy(kpe_hbm.at[layer, pg], kpe_buf.at[slot, i], sem.at[1, slot]))
    return copies


def _pad_new_keys(kn_ref):
    return jnp.concatenate([kn_ref[...], jnp.zeros((LANE - SUBLANE, QK_PAD), _F32)], axis=0).astype(_BF)


def _attn_sample_kernel(pt_ref, q_ref, kn_ref, knp_ref, mg_ref, wuv_ref, ckv_hbm, kpe_hbm, o_ref,
                        ckv_buf, kpe_buf, kbf_a, kbf_b, p_a, p_b, pn_a, pn_b, l_a, l_b, sem, *, layer, n_pages):
    b = pl.program_id(0)
    nreq = pl.num_programs(0) - 1
    slot = b & 1
    rows = MLA_HEADS * SUBLANE
    fetch = functools.partial(_page_copies, pt_ref, ckv_hbm, kpe_hbm, ckv_buf, kpe_buf, sem,
                              layer=layer, n_pages=n_pages)

    @pl.when(b == 0)
    def _():
        for c in fetch(req=b, slot=slot):
            c.start()
        kbf_b[...] = jnp.zeros(kbf_b.shape, _BF)
        p_b[...] = jnp.zeros(p_b.shape, _BF)
        pn_b[...] = jnp.zeros(pn_b.shape, _BF)
        l_b[...] = jnp.ones(l_b.shape, _F32)

    @pl.when(b + 1 < nreq)
    def _():
        for c in fetch(req=b + 1, slot=1 - slot):
            c.start()

    @pl.when(b < nreq)
    def _():
        for c in fetch(req=b, slot=slot):
            c.wait()

    def step(kbf_w, p_w, pn_w, l_w, kbf_r, p_r, pn_r, l_r):
        qb = q_ref[...].reshape(rows, QK_PAD).astype(_BF)
        kbf_w[...] = ckv_buf[slot].astype(_BF)
        kpt = jnp.concatenate([kpe_buf[slot, i] for i in range(n_pages)], axis=1).astype(_BF)
        s = (_dot_nt(qb[:, :KV_LORA], kbf_w[...]) + _dot(qb[:, KV_LORA:KV_LORA + ROPE_DIM], kpt)) * MLA_SCALE
        sn = _dot_nt(qb, _pad_new_keys(kn_ref)) * MLA_SCALE
        tq = lax.broadcasted_iota(jnp.int32, sn.shape, 0) & (SUBLANE - 1)
        tk = lax.broadcasted_iota(jnp.int32, sn.shape, 1)
        sn = jnp.where(tk <= tq, sn, -jnp.inf)
        m = jnp.maximum(jnp.max(s, axis=-1, keepdims=True), jnp.max(sn, axis=-1, keepdims=True))
        p = jnp.exp(s - m)
        pn = jnp.exp(sn - m)
        l_w[...] = jnp.sum(p, axis=-1, keepdims=True) + jnp.sum(pn, axis=-1, keepdims=True)
        p_w[...] = p.astype(_BF)
        pn_w[...] = pn.astype(_BF)
        acc = _dot(p_r[...], kbf_r[...]) + _dot(pn_r[...], _pad_new_keys(knp_ref)[:, :KV_LORA])
        o = (acc / l_r[...]).astype(_BF)
        y = _dot(o, wuv_ref[...])
        mg = mg_ref[...]
        for h in range(MLA_HEADS):
            cs = slice(h * MLA_V_DIM, (h + 1) * MLA_V_DIM)
            yh = y[h * SUBLANE:(h + 1) * SUBLANE, cs]
            o_ref[:, cs] = (yh * _silu(mg[:, cs])).astype(o_ref.dtype)

    @pl.when(slot == 0)
    def _():
        step(kbf_a, p_a, pn_a, l_a, kbf_b, p_b, pn_b, l_b)

    @pl.when(slot == 1)
    def _():
        step(kbf_b, p_b, pn_b, l_b, kbf_a, p_a, pn_a, l_a)


def _attn_sample(q, kcat, u, wuv_all, cache_ckv, cache_kpe_t, pt_flat, layer, n_pages):
    t = kcat.shape[0]
    nreq = t // SUBLANE
    nkeys = n_pages * LANE
    rows = MLA_HEADS * SUBLANE
    cur = lambda b: jnp.minimum(b, nreq - 1)
    prev = lambda b: jnp.maximum(b - 1, 0)
    grid_spec = pltpu.PrefetchScalarGridSpec(
        num_scalar_prefetch=1,
        grid=(nreq + 1,),
        in_specs=[
            pl.BlockSpec((MLA_HEADS, SUBLANE, QK_PAD), lambda b, pt: (0, cur(b), 0)),
            pl.BlockSpec((SUBLANE, QK_PAD), lambda b, pt: (cur(b), 0)),
            pl.BlockSpec((SUBLANE, QK_PAD), lambda b, pt: (prev(b), 0)),
            pl.BlockSpec((SUBLANE, MLA_DIM), lambda b, pt: (prev(b), U_MG // MLA_DIM)),
            pl.BlockSpec((KV_LORA, MLA_DIM), lambda b, pt: (0, 0)),
            pl.BlockSpec(memory_space=pl.ANY),
            pl.BlockSpec(memory_space=pl.ANY),
        ],
        out_specs=pl.BlockSpec((SUBLANE, MLA_DIM), lambda b, pt: (prev(b), 0)),
        scratch_shapes=[
            pltpu.VMEM((2, nkeys, KV_LORA), _F32),
            pltpu.VMEM((2, n_pages, ROPE_DIM, LANE), _F32),
            pltpu.VMEM((nkeys, KV_LORA), _BF), pltpu.VMEM((nkeys, KV_LORA), _BF),
            pltpu.VMEM((rows, nkeys), _BF), pltpu.VMEM((rows, nkeys), _BF),
            pltpu.VMEM((rows, LANE), _BF), pltpu.VMEM((rows, LANE), _BF),
            pltpu.VMEM((rows, 1), _F32), pltpu.VMEM((rows, 1), _F32),
            pltpu.SemaphoreType.DMA((2, 2)),
        ],
    )
    return pl.pallas_call(
        functools.partial(_attn_sample_kernel, layer=layer, n_pages=n_pages),
        grid_spec=grid_spec,
        out_shape=jax.ShapeDtypeStruct((t, MLA_DIM), _F32),
        compiler_params=_cparams(("arbitrary",)),
        name="attn_sample",
    )(pt_flat, q, kcat, kcat, u, wuv_all, cache_ckv, cache_kpe_t)


def _ssd_chunk(xa_ref, dt, a_row, mask, tmat, hprev_fn, d_ref, y_ref, xw_ref, lastmat=None):
    da = dt * a_row
    acum = _exact_dot_01(tmat, da)
    atot = acum[CHUNK - 1:CHUNK, :] if lastmat is None else _exact_dot_01(lastmat, da)
    acum_t = acum.T
    dt_t = dt.T
    ea = jnp.exp(acum)
    w = jnp.exp(atot - acum) * dt
    heads_per_group = SSM_HEADS // SSM_GROUPS
    for g in range(SSM_GROUPS):
        bm = xa_ref[:, SSM_DIM + g * SSM_STATE:SSM_DIM + (g + 1) * SSM_STATE].astype(_BF)
        cm = xa_ref[:, SSM_DIM + (SSM_GROUPS + g) * SSM_STATE:SSM_DIM + (SSM_GROUPS + g + 1) * SSM_STATE].astype(_BF)
        gmat = _dot_nt(cm, bm)
        yo = hprev_fn(g, cm)
        for hh in range(heads_per_group):
            h = g * heads_per_group + hh
            hs = slice(h * SSM_HEAD_DIM, (h + 1) * SSM_HEAD_DIM)
            seg = acum[:, h:h + 1] - acum_t[h:h + 1, :]
            lmat = jnp.exp(jnp.where(mask, seg, -jnp.inf))
            sc = gmat * (lmat * dt_t[h:h + 1, :])
            xh = xa_ref[:, hs]
            yd = _dot(sc.astype(_BF), xh.astype(_BF))
            yoh = yo[:, hh * SSM_HEAD_DIM:(hh + 1) * SSM_HEAD_DIM] * ea[:, h:h + 1]
            y_ref[:, hs] = yd + yoh + xh * d_ref[:, hs]
            xw_ref[:, hs] = xh * w[:, h:h + 1]
    return acum, atot


def _ssd_gate_norm(y, z, ng_ref, o_ref):
    gated = y * _silu(z)
    gw = SSM_DIM // SSM_GROUPS
    for g in range(SSM_GROUPS):
        cs = slice(g * gw, (g + 1) * gw)
        gg = gated[:, cs]
        ms = jnp.mean(gg * gg, axis=-1, keepdims=True)
        o_ref[:, cs] = (gg * lax.rsqrt(ms + EPS) * ng_ref[:, cs]).astype(o_ref.dtype)


def _short_conv_silu(xp_ref, cw_ref, cb_ref, xa_ref, lead):
    for c in range(XBC_DIM // LANE):
        cs = slice(c * LANE, (c + 1) * LANE)
        acc = jnp.zeros((CHUNK, LANE), _F32) + cb_ref[:, cs]
        for k in range(SSM_CONV):
            off = lead - (SSM_CONV - 1) + k
            acc = acc + cw_ref[k:k + 1, cs] * xp_ref[off:off + CHUNK, cs]
        xa_ref[:, cs] = _silu(acc)


def _ssd_prompt_kernel(xbc_ref, z_ref, dt_ref, cw_ref, cb_ref, dtb_ref, a_ref, d_ref, ng_ref,
                       o_ref, hout_ref, xp_ref, xa_ref, y_ref, xw_ref, h_ref, *, l_valid):
    c = pl.program_id(1)

    @pl.when(c == 0)
    def _():
        xp_ref[0:SUBLANE, :] = jnp.zeros((SUBLANE, XBC_DIM), _F32)
        h_ref[...] = jnp.zeros(h_ref.shape, _F32)

    xp_ref[SUBLANE:SUBLANE + CHUNK, :] = xbc_ref[...]
    _short_conv_silu(xp_ref, cw_ref, cb_ref, xa_ref, SUBLANE)
    xp_ref[0:SUBLANE, :] = xp_ref[CHUNK:CHUNK + SUBLANE, :]

    ri = lax.broadcasted_iota(jnp.int32, (CHUNK, CHUNK), 0)
    ci = lax.broadcasted_iota(jnp.int32, (CHUNK, CHUNK), 1)
    mask = ci <= ri
    tmat = jnp.where(mask, 1.0, 0.0).astype(_BF)
    pos = c * CHUNK + lax.broadcasted_iota(jnp.int32, (CHUNK, LANE), 0)
    dt = jnp.where(pos < l_valid, _softplus(dt_ref[...] + dtb_ref[...]), 0.0)

    hb = h_ref[...].astype(_BF)
    gw = SSM_DIM // SSM_GROUPS

    def hprev_fn(g, cm):
        return _dot_nt(cm, hb[g * gw:(g + 1) * gw, :])

    acum, atot = _ssd_chunk(xa_ref, dt, a_ref[...], mask, tmat, hprev_fn, d_ref, y_ref, xw_ref)
    _ssd_gate_norm(y_ref[...], z_ref[...], ng_ref, o_ref)

    xw_t = xw_ref[...].T
    dec = jnp.exp(atot)
    for g in range(SSM_GROUPS):
        bm = xa_ref[:, SSM_DIM + g * SSM_STATE:SSM_DIM + (g + 1) * SSM_STATE].astype(_BF)
        st = _dot(xw_t[g * gw:(g + 1) * gw, :].astype(_BF), bm)
        for hh in range(SSM_HEADS // SSM_GROUPS):
            h = g * (SSM_HEADS // SSM_GROUPS) + hh
            hs = slice(h * SSM_HEAD_DIM, (h + 1) * SSM_HEAD_DIM)
            h_ref[hs, :] = h_ref[hs, :] * dec[:, h:h + 1] + st[hh * SSM_HEAD_DIM:(hh + 1) * SSM_HEAD_DIM, :]

    @pl.when(c == pl.num_programs(1) - 1)
    def _():
        hout_ref[...] = h_ref[...]


def _ssd_prompt(u, lw, nb, lp, l_valid):
    t = u.shape[0]
    nc = lp // CHUNK
    row = lambda b, i: b * nc + i
    full = lambda shape: pl.BlockSpec(shape, lambda b, i: (0,) * len(shape))
    return pl.pallas_call(
        functools.partial(_ssd_prompt_kernel, l_valid=l_valid),
        grid=(nb, nc),
        in_specs=[
            pl.BlockSpec((CHUNK, XBC_DIM), lambda b, i: (row(b, i), U_XBC // XBC_DIM)),
            pl.BlockSpec((CHUNK, SSM_DIM), lambda b, i: (row(b, i), U_Z // SSM_DIM)),
            pl.BlockSpec((CHUNK, LANE), lambda b, i: (row(b, i), U_DT // LANE)),
            full((SSM_CONV, XBC_DIM)), full((1, XBC_DIM)), full((1, LANE)), full((1, LANE)),
            full((1, SSM_DIM)), full((1, SSM_DIM)),
        ],
        out_specs=[
            pl.BlockSpec((CHUNK, SSM_DIM), lambda b, i: (row(b, i), 0)),
            pl.BlockSpec((None, SSM_DIM, SSM_STATE), lambda b, i: (b, 0, 0)),
        ],
        out_shape=[jax.ShapeDtypeStruct((t, SSM_DIM), _BF), jax.ShapeDtypeStruct((nb, SSM_DIM, SSM_STATE), _F32)],
        scratch_shapes=[
            pltpu.VMEM((CHUNK + SUBLANE, XBC_DIM), _F32), pltpu.VMEM((CHUNK, XBC_DIM), _F32),
            pltpu.VMEM((CHUNK, SSM_DIM), _F32), pltpu.VMEM((CHUNK, SSM_DIM), _F32),
            pltpu.VMEM((SSM_DIM, SSM_STATE), _F32),
        ],
        compiler_params=_cparams(("parallel", "arbitrary")),
        name="ssd_prompt",
    )(u, u, u, lw["scw"], lw["scb"], lw["dtb"], lw["a"], lw["dskip"], lw["sng"])


def _ssd_sample_kernel(xbc_ref, z_ref, dt_ref, left_ref, h0_ref, cw_ref, cb_ref, dtb_ref, a_ref, d_ref, ng_ref,
                       o_ref, hout_ref, xp_ref, xa_ref, y_ref, xw_ref, yo_ref):
    xp_ref[:, 0:SUBLANE, :] = left_ref[...]
    xp_ref[:, SUBLANE:2 * SUBLANE, :] = xbc_ref[...].reshape(REQ_BLK, SUBLANE, XBC_DIM)
    for c in range(XBC_DIM // LANE):
        cs = slice(c * LANE, (c + 1) * LANE)
        acc = jnp.zeros((REQ_BLK, SUBLANE, LANE), _F32) + cb_ref[:, cs][None]
        for k in range(SSM_CONV):
            off = SUBLANE - (SSM_CONV - 1) + k
            acc = acc + cw_ref[k:k + 1, cs][None] * xp_ref[:, off:off + SUBLANE, cs]
        xa_ref[:, cs] = _silu(acc).reshape(CHUNK, LANE)

    ri = lax.broadcasted_iota(jnp.int32, (CHUNK, CHUNK), 0)
    ci = lax.broadcasted_iota(jnp.int32, (CHUNK, CHUNK), 1)
    same = (ri >> 3) == (ci >> 3)
    mask = jnp.logical_and(same, ci <= ri)
    tmat = jnp.where(mask, 1.0, 0.0).astype(_BF)
    lastmat = jnp.where(same, 1.0, 0.0).astype(_BF)
    dt = _softplus(dt_ref[...] + dtb_ref[...])
    gw = SSM_DIM // SSM_GROUPS

    for r in range(REQ_BLK):
        rs = slice(r * SUBLANE, (r + 1) * SUBLANE)
        for g in range(SSM_GROUPS):
            cm = xa_ref[rs, SSM_DIM + (SSM_GROUPS + g) * SSM_STATE:SSM_DIM + (SSM_GROUPS + g + 1) * SSM_STATE]
            hb = h0_ref[r, g * gw:(g + 1) * gw, :].astype(_BF)
            yo_ref[rs, g * gw:(g + 1) * gw] = _dot_nt(cm.astype(_BF), hb)

    def hprev_fn(g, cm):
        return yo_ref[:, g * gw:(g + 1) * gw]

    acum, atot = _ssd_chunk(xa_ref, dt, a_ref[...], mask, tmat, hprev_fn, d_ref, y_ref, xw_ref, lastmat=lastmat)
    _ssd_gate_norm(y_ref[...], z_ref[...], ng_ref, o_ref)

    xw_t = xw_ref[...].T
    dec = jnp.exp(atot)
    lane_req = lax.broadcasted_iota(jnp.int32, (gw, CHUNK), 1) >> 3
    for g in range(SSM_GROUPS):
        bm = xa_ref[:, SSM_DIM + g * SSM_STATE:SSM_DIM + (g + 1) * SSM_STATE].astype(_BF)
        xg = xw_t[g * gw:(g + 1) * gw, :]
        for r in range(REQ_BLK):
            st = _dot(jnp.where(lane_req == r, xg, 0.0).astype(_BF), bm)
            for hh in range(SSM_HEADS // SSM_GROUPS):
                h = g * (SSM_HEADS // SSM_GROUPS) + hh
                hs = slice(h * SSM_HEAD_DIM, (h + 1) * SSM_HEAD_DIM)
                d = dec[r * SUBLANE:r * SUBLANE + 1, h:h + 1]
                hout_ref[r, hs, :] = h0_ref[r, hs, :] * d + st[hh * SSM_HEAD_DIM:(hh + 1) * SSM_HEAD_DIM, :]


def _ssd_sample(u, left8, h0_all, layer, lw):
    t = u.shape[0]
    nreq = t // SUBLANE
    full = lambda shape: pl.BlockSpec(shape, lambda i: (0,) * len(shape))
    return pl.pallas_call(
        _ssd_sample_kernel,
        grid=(nreq // REQ_BLK,),
        in_specs=[
            pl.BlockSpec((CHUNK, XBC_DIM), lambda i: (i, U_XBC // XBC_DIM)),
            pl.BlockSpec((CHUNK, SSM_DIM), lambda i: (i, U_Z // SSM_DIM)),
            pl.BlockSpec((CHUNK, LANE), lambda i: (i, U_DT // LANE)),
            pl.BlockSpec((REQ_BLK, SUBLANE, XBC_DIM), lambda i: (i, 0, 0)),
            pl.BlockSpec((None, REQ_BLK, SSM_DIM, SSM_STATE), lambda i: (layer, i, 0, 0)),
            full((SSM_CONV, XBC_DIM)), full((1, XBC_DIM)), full((1, LANE)), full((1, LANE)),
            full((1, SSM_DIM)), full((1, SSM_DIM)),
        ],
        out_specs=[
            pl.BlockSpec((CHUNK, SSM_DIM), lambda i: (i, 0)),
            pl.BlockSpec((REQ_BLK, SSM_DIM, SSM_STATE), lambda i: (i, 0, 0)),
        ],
        out_shape=[jax.ShapeDtypeStruct((t, SSM_DIM), _BF), jax.ShapeDtypeStruct((nreq, SSM_DIM, SSM_STATE), _F32)],
        scratch_shapes=[
            pltpu.VMEM((REQ_BLK, 2 * SUBLANE, XBC_DIM), _F32), pltpu.VMEM((CHUNK, XBC_DIM), _F32),
            pltpu.VMEM((CHUNK, SSM_DIM), _F32), pltpu.VMEM((CHUNK, SSM_DIM), _F32),
            pltpu.VMEM((CHUNK, SSM_DIM), _F32),
        ],
        compiler_params=_cparams(("parallel",)),
        name="ssd_sample",
    )(u, u, u, left8, h0_all, lw["scw"], lw["scb"], lw["dtb"], lw["a"], lw["dskip"], lw["sng"])


def _outproj_kernel(cy_ref, my_ref, sy_ref, h_ref, w_ref, o_ref, mix_ref):
    @pl.when(pl.program_id(1) == 0)
    def _():
        mix_ref[:, 0:CONV_DIM] = cy_ref[...].astype(_BF)
        mix_ref[:, CONV_DIM:CONV_DIM + MLA_DIM] = my_ref[...].astype(_BF)
        mix_ref[:, CONV_DIM + MLA_DIM:] = sy_ref[...].astype(_BF)

    o_ref[...] = h_ref[...] + _dot(mix_ref[...], w_ref[...])


def _outproj(cy, my, sy, h, w_all, layer, tm, tn=D_MODEL):
    t = h.shape[0]
    return pl.pallas_call(
        _outproj_kernel,
        grid=(t // tm, D_MODEL // tn),
        in_specs=[
            pl.BlockSpec((tm, CONV_DIM), lambda i, j: (i, 0)),
            pl.BlockSpec((tm, MLA_DIM), lambda i, j: (i, 0)),
            pl.BlockSpec((tm, SSM_DIM), lambda i, j: (i, 0)),
            pl.BlockSpec((tm, tn), lambda i, j: (i, j)),
            pl.BlockSpec((None, D_MODEL, tn), lambda i, j: (layer, 0, j)),
        ],
        out_specs=pl.BlockSpec((tm, tn), lambda i, j: (i, j)),
        out_shape=jax.ShapeDtypeStruct((t, D_MODEL), _F32),
        scratch_shapes=[pltpu.VMEM((tm, D_MODEL), _BF)],
        compiler_params=_cparams(("parallel", "arbitrary")),
        name="outproj",
    )(cy, my, sy, h, w_all)


def _final_norm_kernel(x_ref, g_ref, o_ref):
    x = x_ref[...]
    ms = jnp.mean(x * x, axis=-1, keepdims=True)
    o_ref[...] = x * lax.rsqrt(ms + EPS) * g_ref[...]


def _final_norm(h, g, tm):
    t = h.shape[0]
    return pl.pallas_call(
        _final_norm_kernel,
        grid=(t // tm,),
        in_specs=[pl.BlockSpec((tm, D_MODEL), lambda i: (i, 0)), pl.BlockSpec((1, D_MODEL), lambda i: (0, 0))],
        out_specs=pl.BlockSpec((tm, D_MODEL), lambda i: (i, 0)),
        out_shape=jax.ShapeDtypeStruct((t, D_MODEL), _F32),
        compiler_params=_cparams(("parallel",)),
        name="final_norm",
    )(h, g)


def _rot_half_cols(w):
    half = ROPE_DIM // 2
    return jnp.concatenate([-w[..., half:], w[..., :half]], axis=-1)


def _prep_weights(norm_g, w_in, conv_dw_w, conv_dw_b, conv_ln_g, conv_ln_b, conv_pw_w, conv_pw_b, mla_qa_g,
                  mla_wq_b, mla_kva_g, mla_wkv_b, ssm_conv_w, ssm_conv_b, ssm_dt_bias, ssm_A_log, ssm_D,
                  ssm_norm_g, w_out):
    w_in_t = jnp.swapaxes(w_in, 1, 2)
    o = 0
    parts = {}
    for name, width in (("glu", 2 * CONV_DIM), ("cg", CONV_DIM), ("cq", Q_LORA), ("ckv", KV_LORA),
                        ("kpe", ROPE_DIM), ("mg", MLA_DIM), ("z", SSM_DIM), ("xbc", XBC_DIM), ("dt", SSM_HEADS)):
        parts[name] = w_in_t[:, o:o + width, :]
        o += width
    half = ROPE_DIM // 2
    kpe_rot = jnp.concatenate([-parts["kpe"][:, half:], parts["kpe"][:, :half]], axis=1)
    dt_pad = jnp.pad(parts["dt"], ((0, 0), (0, LANE - SSM_HEADS), (0, 0)))
    w_in_p = jnp.concatenate(
        [parts["glu"], parts["mg"], parts["xbc"], parts["cg"], parts["cq"], parts["z"], parts["ckv"],
         parts["kpe"], kpe_rot, dt_pad], axis=1).astype(_BF)

    depth = w_in.shape[0]
    wq_nope = mla_wq_b[..., :NOPE_DIM].reshape(depth, Q_LORA, MLA_HEADS * NOPE_DIM)
    wq_rope = mla_wq_b[..., NOPE_DIM:]
    wq_slots = jnp.concatenate([wq_rope, _rot_half_cols(wq_rope)], axis=-1).reshape(depth, Q_LORA, MLA_HEADS * LANE)
    wq = jnp.concatenate([wq_nope, wq_slots], axis=-1).astype(_BF)
    wuk = jnp.transpose(mla_wkv_b[..., :NOPE_DIM], (0, 2, 3, 1)).astype(_BF)
    wqt = jnp.swapaxes(wq, 1, 2)
    wukt = jnp.transpose(mla_wkv_b[..., :NOPE_DIM], (0, 2, 1, 3)).astype(_BF)
    wuvt = jnp.transpose(mla_wkv_b[..., NOPE_DIM:], (0, 2, 3, 1)).astype(_BF)
    wuv_all = mla_wkv_b[..., NOPE_DIM:].reshape(depth, KV_LORA, MLA_DIM).astype(_BF)

    def row(x):
        return x[:, None, :].astype(_F32)

    pad_heads = lambda x: jnp.pad(x, ((0, 0), (0, LANE - SSM_HEADS)))
    layers = []
    for l in range(depth):
        layers.append({
            "norm_g": row(norm_g)[l],
            "dww": conv_dw_w[l], "dwb": row(conv_dw_b)[l], "lng": row(conv_ln_g)[l], "lnb": row(conv_ln_b)[l],
            "pww": conv_pw_w[l].astype(_BF), "pwb": row(conv_pw_b)[l],
            "qag": row(mla_qa_g)[l], "wq": wq[l], "wuk": wuk[l], "kvg": row(mla_kva_g)[l],
            "wqt": wqt[l], "wukt": wukt[l], "wuvt": wuvt[l], "wuv_all": wuv_all[l],
            "scw": ssm_conv_w[l], "scb": row(ssm_conv_b)[l],
            "dtb": row(pad_heads(ssm_dt_bias))[l], "a": row(pad_heads(-jnp.exp(ssm_A_log.astype(_F32))))[l],
            "dskip": row(jnp.repeat(ssm_D, SSM_HEAD_DIM, axis=-1))[l], "sng": row(ssm_norm_g)[l],
        })
    return layers, w_in_p, w_out.astype(_BF)


def _rope_table(pos):
    inv = ROPE_THETA ** (-jnp.arange(0, ROPE_DIM, 2, dtype=_F32) / ROPE_DIM)
    ang = pos.astype(_F32)[:, None] * inv[None, :]
    c, s = jnp.cos(ang), jnp.sin(ang)
    return jnp.concatenate([c, c, s, s], axis=-1)


def _largest_tile(t, cands):
    for c in cands:
        if t % c == 0:
            return c
    raise ValueError(f"no row tile for {t}")


def kernel(x_prompt, x_sample, cache_ckv, cache_kpe, page_table, state_conv, state_ssm_conv, state_ssm, meta_tokens, norm_g, w_in, conv_dw_w, conv_dw_b, conv_ln_g, conv_ln_b, conv_pw_w, conv_pw_b, mla_qa_g, mla_wq_b, mla_kva_g, mla_wkv_b, ssm_conv_w, ssm_conv_b, ssm_dt_bias, ssm_A_log, ssm_D, ssm_norm_g, w_out, final_g):
    nb, seq, _ = x_prompt.shape
    nreq, dseq, _ = x_sample.shape
    depth = w_in.shape[0]
    n_pages = page_table.shape[1]
    page_size = cache_ckv.shape[2]
    assert dseq == SUBLANE and nreq % REQ_BLK == 0 and page_size == LANE
    l_valid = N_META + seq
    lp = -(-l_valid // ATT_BLK) * ATT_BLK
    tp, ts = nb * lp, nreq * dseq

    layers, w_in_all, w_out_all = _prep_weights(
        norm_g, w_in, conv_dw_w, conv_dw_b, conv_ln_g, conv_ln_b, conv_pw_w, conv_pw_b, mla_qa_g, mla_wq_b,
        mla_kva_g, mla_wkv_b, ssm_conv_w, ssm_conv_b, ssm_dt_bias, ssm_A_log, ssm_D, ssm_norm_g, w_out)

    meta = jnp.broadcast_to(meta_tokens[None].astype(_F32), (nb, N_META, D_MODEL))
    hp = jnp.concatenate([meta, x_prompt, jnp.zeros((nb, lp - l_valid, D_MODEL), _F32)], axis=1).reshape(tp, D_MODEL)
    hs = x_sample.reshape(ts, D_MODEL)
    cs_p = jnp.tile(_rope_table(jnp.arange(lp, dtype=jnp.int32)), (nb, 1))
    cst_p = cs_p.T
    past_len = n_pages * page_size
    cs_s = jnp.tile(_rope_table(past_len + jnp.arange(dseq, dtype=jnp.int32)), (nreq, 1))
    pt_flat = page_table.reshape(-1).astype(jnp.int32)
    cache_kpe_t = jnp.swapaxes(cache_kpe, 2, 3)
    left_conv = jnp.pad(state_conv, ((0, 0), (0, 0), (2, 0), (0, 0)))
    left_ssm = jnp.pad(state_ssm_conv, ((0, 0), (0, 0), (SUBLANE - (SSM_CONV - 1), 0), (0, 0)))
    h0_all = state_ssm.reshape(depth, nreq, SSM_DIM, SSM_STATE)

    tm_p = _largest_tile(tp, (768, 512, 384, 256, 128))
    tm_s = _largest_tile(ts, (512, 256, 128))
    tq_p = _largest_tile(tp, (384, 256, 128))
    tq_s = _largest_tile(ts, (256, 128))

    st_p = [[] for _ in range(5)]
    st_s = [[] for _ in range(5)]
    for l in range(depth):
        lw = layers[l]
        u = _inproj(hp, lw["norm_g"], w_in_all, l, tm_p)
        v, cy = _conv_prompt(u, lw, nb, lp)
        qt, kcat, vt, ckv, kpe = _qkt(u, cs_p, cst_p, lw, nb, lp)
        my = _attn_prompt(qt, kcat, vt, u, lw["wuvt"], nb, lp)
        sy, hfin = _ssd_prompt(u, lw, nb, lp, l_valid)
        hp = _outproj(cy, my, sy, hp, w_out_all, l, tm_p)
        u3 = u.reshape(nb, lp, U_TOTAL)
        st_p[0].append(ckv.reshape(nb, lp, KV_LORA)[:, :l_valid])
        st_p[1].append(kpe.reshape(nb, lp, ROPE_DIM)[:, :l_valid])
        st_p[2].append(v.reshape(nb, lp, CONV_DIM)[:, l_valid - (CONV_WIDTH - 1):l_valid])
        st_p[3].append(u3[:, l_valid - (SSM_CONV - 1):l_valid, U_XBC:U_XBC + XBC_DIM])
        st_p[4].append(hfin.reshape(nb, SSM_HEADS, SSM_HEAD_DIM, SSM_STATE))
        u = _inproj(hs, lw["norm_g"], w_in_all, l, tm_s)
        cst, cy = _conv_sample(u, left_conv[l], lw)
        q, kcat, ckv, kpe = _qk(u, cs_s, lw, tq_s, _F32)
        my = _attn_sample(q, kcat, u, lw["wuv_all"], cache_ckv, cache_kpe_t, pt_flat, l, n_pages)
        sy, hnew = _ssd_sample(u, left_ssm[l], h0_all, l, lw)
        hs = _outproj(cy, my, sy, hs, w_out_all, l, tm_s)
        u3 = u.reshape(nreq, dseq, U_TOTAL)
        st_s[0].append(ckv.reshape(nreq, dseq, KV_LORA))
        st_s[1].append(kpe.reshape(nreq, dseq, ROPE_DIM))
        st_s[2].append(cst[:, 2:, :])
        st_s[3].append(u3[:, dseq - (SSM_CONV - 1):, U_XBC:U_XBC + XBC_DIM])
        st_s[4].append(hnew.reshape(nreq, SSM_HEADS, SSM_HEAD_DIM, SSM_STATE))

    y_prompt = _final_norm(hp, final_g[None, :].astype(_F32), tm_p).reshape(nb, lp, D_MODEL)[:, N_META:l_valid]
    y_sample = _final_norm(hs, final_g[None, :].astype(_F32), tm_s).reshape(nreq, dseq, D_MODEL)
    outs_p = [jnp.stack(s) for s in st_p]
    outs_s = [jnp.stack(s) for s in st_s]
    return (y_prompt, y_sample, *outs_p, *outs_s)
```

```python
import functools
import math

import jax
import jax.numpy as jnp
from jax import lax
from jax.experimental import pallas as pl
from jax.experimental.pallas import tpu as pltpu

D_MODEL = 2048
DEPTH = 4
N_META = 16
CONV_DIM = 512
CONV_WIDTH = 31
MLA_HEADS = 8
MLA_V_DIM = 128
MLA_DIM = MLA_HEADS * MLA_V_DIM
Q_LORA = 512
KV_LORA = 256
NOPE_DIM = 128
ROPE_DIM = 64
ROPE_THETA = 10000.0
MLA_SCALE = 1.0 / math.sqrt(NOPE_DIM + ROPE_DIM)
SSM_DIM = 512
SSM_HEAD_DIM = 64
SSM_HEADS = 8
SSM_GROUPS = 2
SSM_STATE = 128
SSM_CONV = 4
XBC_DIM = SSM_DIM + 2 * SSM_GROUPS * SSM_STATE
EPS = 1e-6

LANE = 128
SUBLANE = 8
VMEM_LIMIT = 48 * 1024 * 1024

U_GLU = 0
U_MG = 1024
U_XBC = 2048
U_CG = 3072
U_CQ = 3584
U_Z = 4096
U_CKV = 4608
U_KPE = 4864
U_DT = 4992
U_TOTAL = 5120

QK_PAD = 384
CHUNK = 128
ATT_BLK = 256
ATT_GROUP = 4
EXP2_SCALE = MLA_SCALE * math.log2(math.e)
_BF = jnp.bfloat16
_F32 = jnp.float32


def _cparams(sem):
    return pltpu.CompilerParams(dimension_semantics=sem, vmem_limit_bytes=VMEM_LIMIT)


def _dot(a, b):
    return jnp.dot(a, b, preferred_element_type=_F32)


def _dot_nt(a, b):
    return lax.dot_general(a, b, (((1,), (1,)), ((), ())), preferred_element_type=_F32)


def _sigmoid(x):
    return 1.0 / (1.0 + jnp.exp(-x))


def _silu(x):
    return x * _sigmoid(x)


def _softplus(x):
    return jnp.maximum(x, 0.0) + jnp.log(1.0 + jnp.exp(-jnp.abs(x)))


def _exact_dot_01(mat01_bf16, x):
    hi = x.astype(_BF)
    r1 = x - hi.astype(_F32)
    mid = r1.astype(_BF)
    lo = (r1 - mid.astype(_F32)).astype(_BF)
    return _dot(mat01_bf16, hi) + _dot(mat01_bf16, mid) + _dot(mat01_bf16, lo)


def _inproj_kernel(x_ref, g_ref, w_ref, o_ref, xn_ref):
    @pl.when(pl.program_id(1) == 0)
    def _():
        x = x_ref[...]
        ms = jnp.mean(x * x, axis=-1, keepdims=True)
        xn_ref[...] = (x * lax.rsqrt(ms + EPS) * g_ref[...]).astype(_BF)

    o_ref[...] = _dot_nt(xn_ref[...], w_ref[...])


def _inproj(h, g, w_all, layer, tm, tn=U_TOTAL // 2):
    t = h.shape[0]
    n = w_all.shape[1]
    return pl.pallas_call(
        _inproj_kernel,
        grid=(t // tm, n // tn),
        in_specs=[
            pl.BlockSpec((tm, D_MODEL), lambda i, j: (i, 0)),
            pl.BlockSpec((1, D_MODEL), lambda i, j: (0, 0)),
            pl.BlockSpec((None, tn, D_MODEL), lambda i, j: (layer, j, 0)),
        ],
        out_specs=pl.BlockSpec((tm, tn), lambda i, j: (i, j)),
        out_shape=jax.ShapeDtypeStruct((t, n), _F32),
        scratch_shapes=[pltpu.VMEM((tm, D_MODEL), _BF)],
        compiler_params=_cparams(("parallel", "arbitrary")),
        name="inproj",
    )(h, g, w_all)


def _conv_tail(y, cg, lng_ref, lnb_ref, pww_ref, pwb_ref):
    mu = jnp.mean(y, axis=-1, keepdims=True)
    yc = y - mu
    var = jnp.mean(yc * yc, axis=-1, keepdims=True)
    yn = yc * lax.rsqrt(var + EPS) * lng_ref[...] + lnb_ref[...]
    s = _silu(yn)
    o = _dot(s.astype(_BF), pww_ref[...]) + pwb_ref[...]
    return o * _silu(cg)


def _conv_prompt_kernel(glu_ref, cg_ref, dww_ref, dwb_ref, lng_ref, lnb_ref, pww_ref, pwb_ref,
                        v_ref, cy_ref, vp_ref, y_ref, sh_ref):
    @pl.when(pl.program_id(1) == 0)
    def _():
        vp_ref[0:32, :] = jnp.zeros((32, CONV_DIM), _F32)

    v = glu_ref[:, :CONV_DIM] * _sigmoid(glu_ref[:, CONV_DIM:])
    v_ref[...] = v
    vp_ref[32:32 + CHUNK, :] = v
    for c in range(CONV_DIM // LANE):
        cs = slice(c * LANE, (c + 1) * LANE)
        acc = jnp.zeros((CHUNK, LANE), _F32)
        for r in range(SUBLANE):
            taps = range(r, CONV_WIDTH, SUBLANE)
            span = CHUNK + SUBLANE * (len(taps) - 1)
            sh_ref[0:span, :] = vp_ref[2 + r:2 + r + span, cs]
            for j, k in enumerate(taps):
                acc = acc + dww_ref[k:k + 1, cs] * sh_ref[SUBLANE * j:SUBLANE * j + CHUNK, :]
        y_ref[:, cs] = acc + dwb_ref[:, cs]
    cy = _conv_tail(y_ref[...], cg_ref[...], lng_ref, lnb_ref, pww_ref, pwb_ref)
    cy_ref[...] = cy.astype(cy_ref.dtype)
    vp_ref[0:32, :] = vp_ref[CHUNK:CHUNK + 32, :]


def _conv_prompt(u, lw, nb, lp):
    t = u.shape[0]
    nt = lp // CHUNK
    row = lambda b, i: b * nt + i
    full = lambda shape: pl.BlockSpec(shape, lambda b, i: (0,) * len(shape))
    return pl.pallas_call(
        _conv_prompt_kernel,
        grid=(nb, nt),
        in_specs=[
            pl.BlockSpec((CHUNK, 2 * CONV_DIM), lambda b, i: (row(b, i), U_GLU // (2 * CONV_DIM))),
            pl.BlockSpec((CHUNK, CONV_DIM), lambda b, i: (row(b, i), U_CG // CONV_DIM)),
            full((CONV_WIDTH, CONV_DIM)), full((1, CONV_DIM)), full((1, CONV_DIM)), full((1, CONV_DIM)),
            full((CONV_DIM, CONV_DIM)), full((1, CONV_DIM)),
        ],
        out_specs=[
            pl.BlockSpec((CHUNK, CONV_DIM), lambda b, i: (row(b, i), 0)),
            pl.BlockSpec((CHUNK, CONV_DIM), lambda b, i: (row(b, i), 0)),
        ],
        out_shape=[jax.ShapeDtypeStruct((t, CONV_DIM), _F32), jax.ShapeDtypeStruct((t, CONV_DIM), _BF)],
        scratch_shapes=[pltpu.VMEM((CHUNK + 32, CONV_DIM), _F32), pltpu.VMEM((CHUNK, CONV_DIM), _F32),
                        pltpu.VMEM((CHUNK + 32, LANE), _F32)],
        compiler_params=_cparams(("parallel", "arbitrary")),
        name="conv_prompt",
    )(u, u, lw["dww"], lw["dwb"], lw["lng"], lw["lnb"], lw["pww"], lw["pwb"])


REQ_BLK = CHUNK // SUBLANE


def _conv_sample_kernel(glu_ref, cg_ref, left_ref, dww_ref, dwb_ref, lng_ref, lnb_ref, pww_ref, pwb_ref,
                        st_ref, cy_ref, vp_ref, y_ref):
    v = glu_ref[:, :CONV_DIM] * _sigmoid(glu_ref[:, CONV_DIM:])
    vp_ref[:, 0:32, :] = left_ref[...]
    vp_ref[:, 32:40, :] = v.reshape(REQ_BLK, SUBLANE, CONV_DIM)
    for c in range(CONV_DIM // LANE):
        cs = slice(c * LANE, (c + 1) * LANE)
        acc = jnp.zeros((REQ_BLK, SUBLANE, LANE), _F32)
        for k in range(CONV_WIDTH):
            acc = acc + dww_ref[k:k + 1, cs][None] * vp_ref[:, 2 + k:2 + k + SUBLANE, cs]
        y_ref[:, :, cs] = acc + dwb_ref[:, cs][None]
    y = y_ref[...].reshape(CHUNK, CONV_DIM)
    cy = _conv_tail(y, cg_ref[...], lng_ref, lnb_ref, pww_ref, pwb_ref)
    cy_ref[...] = cy.astype(cy_ref.dtype)
    st_ref[...] = vp_ref[:, 8:40, :]


def _conv_sample(u, left32, lw):
    t = u.shape[0]
    nreq = t // SUBLANE
    full = lambda shape: pl.BlockSpec(shape, lambda i: (0,) * len(shape))
    return pl.pallas_call(
        _conv_sample_kernel,
        grid=(nreq // REQ_BLK,),
        in_specs=[
            pl.BlockSpec((CHUNK, 2 * CONV_DIM), lambda i: (i, U_GLU // (2 * CONV_DIM))),
            pl.BlockSpec((CHUNK, CONV_DIM), lambda i: (i, U_CG // CONV_DIM)),
            pl.BlockSpec((REQ_BLK, 32, CONV_DIM), lambda i: (i, 0, 0)),
            full((CONV_WIDTH, CONV_DIM)), full((1, CONV_DIM)), full((1, CONV_DIM)), full((1, CONV_DIM)),
            full((CONV_DIM, CONV_DIM)), full((1, CONV_DIM)),
        ],
        out_specs=[
            pl.BlockSpec((REQ_BLK, 32, CONV_DIM), lambda i: (i, 0, 0)),
            pl.BlockSpec((CHUNK, CONV_DIM), lambda i: (i, 0)),
        ],
        out_shape=[jax.ShapeDtypeStruct((nreq, 32, CONV_DIM), _F32), jax.ShapeDtypeStruct((t, CONV_DIM), _BF)],
        scratch_shapes=[pltpu.VMEM((REQ_BLK, 40, CONV_DIM), _F32), pltpu.VMEM((REQ_BLK, SUBLANE, CONV_DIM), _F32)],
        compiler_params=_cparams(("parallel",)),
        name="conv_sample",
    )(u, u, left32, lw["dww"], lw["dwb"], lw["lng"], lw["lnb"], lw["pww"], lw["pwb"])


def _rope_slot(slot, cs):
    r = slot * cs
    r = r + pltpu.roll(r, 64, axis=1)
    lane = lax.broadcasted_iota(jnp.int32, r.shape, 1)
    return jnp.where(lane < ROPE_DIM, r, 0.0)


def _qk_kernel(cq_ref, ckv_ref, kpe_ref, cs_ref, qag_ref, wq_ref, wuk_ref, kvg_ref,
               q_ref, kcat_ref, ckvo_ref, kpeo_ref):
    cq = cq_ref[...]
    ms = jnp.mean(cq * cq, axis=-1, keepdims=True)
    cqn = (cq * lax.rsqrt(ms + EPS) * qag_ref[...]).astype(_BF)
    q = _dot(cqn, wq_ref[...])
    cs = cs_ref[...]
    for h in range(MLA_HEADS):
        qn = q[:, h * NOPE_DIM:(h + 1) * NOPE_DIM].astype(_BF)
        ql = _dot(qn, wuk_ref[h])
        q_ref[h, :, 0:KV_LORA] = ql.astype(q_ref.dtype)
        slot = q[:, MLA_HEADS * NOPE_DIM + h * LANE:MLA_HEADS * NOPE_DIM + (h + 1) * LANE]
        q_ref[h, :, KV_LORA:QK_PAD] = _rope_slot(slot, cs).astype(q_ref.dtype)
    c = ckv_ref[...]
    ms = jnp.mean(c * c, axis=-1, keepdims=True)
    ckv = c * lax.rsqrt(ms + EPS) * kvg_ref[...]
    ckvo_ref[...] = ckv
    kcat_ref[:, 0:KV_LORA] = ckv.astype(kcat_ref.dtype)
    kr = _rope_slot(kpe_ref[...], cs)
    kcat_ref[:, KV_LORA:QK_PAD] = kr.astype(kcat_ref.dtype)
    kpeo_ref[...] = kr[:, :ROPE_DIM]


def _qk(u, cs, lw, tm, qdtype):
    t = u.shape[0]
    full = lambda shape: pl.BlockSpec(shape, lambda i: (0,) * len(shape))
    return pl.pallas_call(
        _qk_kernel,
        grid=(t // tm,),
        in_specs=[
            pl.BlockSpec((tm, Q_LORA), lambda i: (i, U_CQ // Q_LORA)),
            pl.BlockSpec((tm, KV_LORA), lambda i: (i, U_CKV // KV_LORA)),
            pl.BlockSpec((tm, LANE), lambda i: (i, U_KPE // LANE)),
            pl.BlockSpec((tm, LANE), lambda i: (i, 0)),
            full((1, Q_LORA)), full((Q_LORA, 2 * MLA_HEADS * NOPE_DIM)),
            full((MLA_HEADS, NOPE_DIM, KV_LORA)), full((1, KV_LORA)),
        ],
        out_specs=[
            pl.BlockSpec((MLA_HEADS, tm, QK_PAD), lambda i: (0, i, 0)),
            pl.BlockSpec((tm, QK_PAD), lambda i: (i, 0)),
            pl.BlockSpec((tm, KV_LORA), lambda i: (i, 0)),
            pl.BlockSpec((tm, ROPE_DIM), lambda i: (i, 0)),
        ],
        out_shape=[
            jax.ShapeDtypeStruct((MLA_HEADS, t, QK_PAD), qdtype),
            jax.ShapeDtypeStruct((t, QK_PAD), qdtype),
            jax.ShapeDtypeStruct((t, KV_LORA), _F32),
            jax.ShapeDtypeStruct((t, ROPE_DIM), _F32),
        ],
        compiler_params=_cparams(("parallel",)),
        name="qk",
    )(u, u, u, cs, lw["qag"], lw["wq"], lw["wuk"], lw["kvg"])


def _qkt_kernel(cq_ref, ckv_ref, kpe_ref, cs_ref, cst_ref, qag_ref, wqt_ref, wukt_ref, kvg_ref,
                qt_ref, kcat_ref, vt_ref, ckvo_ref, kpeo_ref):
    tm = ATT_BLK
    cq = cq_ref[...]
    ms = jnp.mean(cq * cq, axis=-1, keepdims=True)
    cqn_t = (cq * lax.rsqrt(ms + EPS) * qag_ref[...]).T.astype(_BF)
    qt = _dot(wqt_ref[...], cqn_t)
    cst = cst_ref[...]
    for h in range(MLA_HEADS):
        cols = slice(h * tm, (h + 1) * tm)
        qn_t = qt[h * NOPE_DIM:(h + 1) * NOPE_DIM].astype(_BF)
        qt_ref[0:KV_LORA, cols] = _dot(wukt_ref[h], qn_t).astype(qt_ref.dtype)
        r = qt[MLA_HEADS * NOPE_DIM + h * LANE:MLA_HEADS * NOPE_DIM + (h + 1) * LANE] * cst
        qt_ref[KV_LORA:KV_LORA + ROPE_DIM, cols] = (r[:ROPE_DIM] + r[ROPE_DIM:]).astype(qt_ref.dtype)
        qt_ref[KV_LORA + ROPE_DIM:QK_PAD, cols] = jnp.zeros((QK_PAD - KV_LORA - ROPE_DIM, tm), qt_ref.dtype)
    c = ckv_ref[...]
    ms = jnp.mean(c * c, axis=-1, keepdims=True)
    ckv = c * lax.rsqrt(ms + EPS) * kvg_ref[...]
    ckvo_ref[...] = ckv
    kcat_ref[:, 0:KV_LORA] = ckv.astype(kcat_ref.dtype)
    vt_ref[...] = ckv.T.astype(vt_ref.dtype)
    kr = _rope_slot(kpe_ref[...], cs_ref[...])
    kcat_ref[:, KV_LORA:QK_PAD] = kr.astype(kcat_ref.dtype)
    kpeo_ref[...] = kr[:, :ROPE_DIM]


def _qkt(u, cs, cst, lw, nb, lp):
    t = u.shape[0]
    tm = ATT_BLK
    nblk = t // tm
    full = lambda shape: pl.BlockSpec(shape, lambda i: (0,) * len(shape))
    return pl.pallas_call(
        _qkt_kernel,
        grid=(nblk,),
        in_specs=[
            pl.BlockSpec((tm, Q_LORA), lambda i: (i, U_CQ // Q_LORA)),
            pl.BlockSpec((tm, KV_LORA), lambda i: (i, U_CKV // KV_LORA)),
            pl.BlockSpec((tm, LANE), lambda i: (i, U_KPE // LANE)),
            pl.BlockSpec((tm, LANE), lambda i: (i, 0)),
            pl.BlockSpec((LANE, tm), lambda i: (0, i)),
            full((1, Q_LORA)), full((2 * MLA_HEADS * NOPE_DIM, Q_LORA)),
            full((MLA_HEADS, KV_LORA, NOPE_DIM)), full((1, KV_LORA)),
        ],
        out_specs=[
            pl.BlockSpec((None, QK_PAD, MLA_HEADS * tm), lambda i: (i, 0, 0)),
            pl.BlockSpec((tm, QK_PAD), lambda i: (i, 0)),
            pl.BlockSpec((None, KV_LORA, tm), lambda i: (i, 0, 0)),
            pl.BlockSpec((tm, KV_LORA), lambda i: (i, 0)),
            pl.BlockSpec((tm, ROPE_DIM), lambda i: (i, 0)),
        ],
        out_shape=[
            jax.ShapeDtypeStruct((nblk, QK_PAD, MLA_HEADS * tm), _BF),
            jax.ShapeDtypeStruct((t, QK_PAD), _BF),
            jax.ShapeDtypeStruct((nblk, KV_LORA, tm), _BF),
            jax.ShapeDtypeStruct((t, KV_LORA), _F32),
            jax.ShapeDtypeStruct((t, ROPE_DIM), _F32),
        ],
        compiler_params=_cparams(("parallel",)),
        name="qkt",
    )(u, u, u, cs, cst, lw["qag"], lw["wqt"], lw["wukt"], lw["kvg"])


def _attn_prompt_kernel(qt_ref, k_ref, vt_ref, mg_ref, wuvt_ref, o_ref, m_ref, l_ref, acc_ref):
    qi = pl.program_id(1)
    ncol = MLA_HEADS * ATT_BLK
    kk = lax.broadcasted_iota(jnp.int32, (ATT_BLK, ATT_BLK), 0)
    qq = lax.broadcasted_iota(jnp.int32, (ATT_BLK, ATT_BLK), 1)
    diag_mask = jnp.concatenate([kk <= qq] * MLA_HEADS, axis=1)

    m_ref[...] = jnp.full((1, ncol), -jnp.inf, _F32)
    l_ref[...] = jnp.zeros((1, ncol), _F32)
    acc_ref[...] = jnp.zeros((KV_LORA, ncol), _F32)

    def group(c, n, masked):
        qt = qt_ref[...]
        ss = []
        for j in range(n):
            kc = k_ref[pl.ds(pl.multiple_of((c + j) * ATT_BLK, ATT_BLK), ATT_BLK), :]
            ss.append(_dot(kc, qt))
        if masked:
            ss[-1] = jnp.where(diag_mask, ss[-1], -jnp.inf)
        m_prev = m_ref[...]
        m_new = m_prev
        for s in ss:
            m_new = jnp.maximum(m_new, jnp.max(s, axis=0, keepdims=True))
        alpha = jnp.exp2((m_prev - m_new) * EXP2_SCALE)
        l_new = alpha * l_ref[...]
        pv = None
        for j, s in enumerate(ss):
            p = jnp.exp2((s - m_new) * EXP2_SCALE)
            l_new = l_new + jnp.sum(p, axis=0, keepdims=True)
            d = _dot(vt_ref[c + j], p.astype(_BF))
            pv = d if pv is None else pv + d
        l_ref[...] = l_new
        acc_ref[...] = alpha * acc_ref[...] + pv
        m_ref[...] = m_new

    def body(cg, carry):
        group(cg * ATT_GROUP, ATT_GROUP, False)
        return carry

    ngroups = qi >> (ATT_GROUP.bit_length() - 1)
    lax.fori_loop(0, ngroups, body, 0)
    base = ngroups * ATT_GROUP
    n = ATT_GROUP // 2
    while n >= 1:
        @pl.when((qi & n) != 0)
        def _(base=base, n=n):
            group(base, n, False)
        base = base + (qi & n)
        n //= 2
    group(qi, 1, True)
    for h in range(MLA_HEADS):
        cols = slice(h * ATT_BLK, (h + 1) * ATT_BLK)
        o_t = (acc_ref[:, cols] / l_ref[:, cols]).astype(_BF)
        y = _dot(wuvt_ref[h], o_t).T
        g = mg_ref[:, h * MLA_V_DIM:(h + 1) * MLA_V_DIM]
        o_ref[:, h * MLA_V_DIM:(h + 1) * MLA_V_DIM] = (y * _silu(g)).astype(o_ref.dtype)


def _attn_prompt(qt, kcat, vt, u, wuvt, nb, lp):
    t = kcat.shape[0]
    nq = lp // ATT_BLK
    return pl.pallas_call(
        _attn_prompt_kernel,
        grid=(nb, nq),
        in_specs=[
            pl.BlockSpec((None, QK_PAD, MLA_HEADS * ATT_BLK), lambda b, i: (b * nq + i, 0, 0)),
            pl.BlockSpec((None, lp, QK_PAD), lambda b, i: (b, 0, 0)),
            pl.BlockSpec((None, nq, KV_LORA, ATT_BLK), lambda b, i: (b, 0, 0, 0)),
            pl.BlockSpec((ATT_BLK, MLA_DIM), lambda b, i: (b * nq + i, U_MG // MLA_DIM)),
            pl.BlockSpec((MLA_HEADS, MLA_V_DIM, KV_LORA), lambda b, i: (0, 0, 0)),
        ],
        out_specs=pl.BlockSpec((ATT_BLK, MLA_DIM), lambda b, i: (b * nq + i, 0)),
        out_shape=jax.ShapeDtypeStruct((t, MLA_DIM), _BF),
        scratch_shapes=[
            pltpu.VMEM((1, MLA_HEADS * ATT_BLK), _F32),
            pltpu.VMEM((1, MLA_HEADS * ATT_BLK), _F32),
            pltpu.VMEM((KV_LORA, MLA_HEADS * ATT_BLK), _F32),
        ],
        compiler_params=_cparams(("parallel", "arbitrary")),
        name="attn_prompt",
    )(qt, kcat.reshape(nb, lp, QK_PAD), vt.reshape(nb, nq, KV_LORA, ATT_BLK), u, wuvt)


def _page_copies(pt_ref, ckv_hbm, kpe_hbm, ckv_buf, kpe_buf, sem, req, slot, layer, n_pages):
    copies = []
    for i in range(n_pages):
        pg = pt_ref[req * n_pages + i]
        copies.append(pltpu.make_async_copy(
            ckv_hbm.at[layer, pg], ckv_buf.at[slot, pl.ds(i * LANE, LANE), :], sem.at[0, slot]))
        copies.append(pltpu.make_async_copy(kpe_hbm.at[layer, pg], kpe_buf.at[slot, i], sem.at[1, slot]))
    return copies


def _pad_new_keys(kn_ref):
    return jnp.concatenate([kn_ref[...], jnp.zeros((LANE - SUBLANE, QK_PAD), _F32)], axis=0).astype(_BF)


def _attn_sample_kernel(pt_ref, q_ref, kn_ref, knp_ref, mg_ref, wuv_ref, ckv_hbm, kpe_hbm, o_ref,
                        ckv_buf, kpe_buf, kbf_a, kbf_b, p_a, p_b, pn_a, pn_b, l_a, l_b, sem, *, layer, n_pages):
    b = pl.program_id(0)
    nreq = pl.num_programs(0) - 1
    slot = b & 1
    rows = MLA_HEADS * SUBLANE
    fetch = functools.partial(_page_copies, pt_ref, ckv_hbm, kpe_hbm, ckv_buf, kpe_buf, sem,
                              layer=layer, n_pages=n_pages)

    @pl.when(b == 0)
    def _():
        for c in fetch(req=b, slot=slot):
            c.start()
        kbf_b[...] = jnp.zeros(kbf_b.shape, _BF)
        p_b[...] = jnp.zeros(p_b.shape, _BF)
        pn_b[...] = jnp.zeros(pn_b.shape, _BF)
        l_b[...] = jnp.ones(l_b.shape, _F32)

    @pl.when(b + 1 < nreq)
    def _():
        for c in fetch(req=b + 1, slot=1 - slot):
            c.start()

    @pl.when(b < nreq)
    def _():
        for c in fetch(req=b, slot=slot):
            c.wait()

    def step(kbf_w, p_w, pn_w, l_w, kbf_r, p_r, pn_r, l_r):
        qb = q_ref[...].reshape(rows, QK_PAD).astype(_BF)
        kbf_w[...] = ckv_buf[slot].astype(_BF)
        kpt = jnp.concatenate([kpe_buf[slot, i] for i in range(n_pages)], axis=1).astype(_BF)
        s = (_dot_nt(qb[:, :KV_LORA], kbf_w[...]) + _dot(qb[:, KV_LORA:KV_LORA + ROPE_DIM], kpt)) * MLA_SCALE
        sn = _dot_nt(qb, _pad_new_keys(kn_ref)) * MLA_SCALE
        tq = lax.broadcasted_iota(jnp.int32, sn.shape, 0) & (SUBLANE - 1)
        tk = lax.broadcasted_iota(jnp.int32, sn.shape, 1)
        sn = jnp.where(tk <= tq, sn, -jnp.inf)
        m = jnp.maximum(jnp.max(s, axis=-1, keepdims=True), jnp.max(sn, axis=-1, keepdims=True))
        p = jnp.exp(s - m)
        pn = jnp.exp(sn - m)
        l_w[...] = jnp.sum(p, axis=-1, keepdims=True) + jnp.sum(pn, axis=-1, keepdims=True)
        p_w[...] = p.astype(_BF)
        pn_w[...] = pn.astype(_BF)
        acc = _dot(p_r[...], kbf_r[...]) + _dot(pn_r[...], _pad_new_keys(knp_ref)[:, :KV_LORA])
        o = (acc / l_r[...]).astype(_BF)
        y = _dot(o, wuv_ref[...])
        mg = mg_ref[...]
        for h in range(MLA_HEADS):
            cs = slice(h * MLA_V_DIM, (h + 1) * MLA_V_DIM)
            yh = y[h * SUBLANE:(h + 1) * SUBLANE, cs]
            o_ref[:, cs] = (yh * _silu(mg[:, cs])).astype(o_ref.dtype)

    @pl.when(slot == 0)
    def _():
        step(kbf_a, p_a, pn_a, l_a, kbf_b, p_b, pn_b, l_b)

    @pl.when(slot == 1)
    def _():
        step(kbf_b, p_b, pn_b, l_b, kbf_a, p_a, pn_a, l_a)


def _attn_sample(q, kcat, u, wuv_all, cache_ckv, cache_kpe_t, pt_flat, layer, n_pages):
    t = kcat.shape[0]
    nreq = t // SUBLANE
    nkeys = n_pages * LANE
    rows = MLA_HEADS * SUBLANE
    cur = lambda b: jnp.minimum(b, nreq - 1)
    prev = lambda b: jnp.maximum(b - 1, 0)
    grid_spec = pltpu.PrefetchScalarGridSpec(
        num_scalar_prefetch=1,
        grid=(nreq + 1,),
        in_specs=[
            pl.BlockSpec((MLA_HEADS, SUBLANE, QK_PAD), lambda b, pt: (0, cur(b), 0)),
            pl.BlockSpec((SUBLANE, QK_PAD), lambda b, pt: (cur(b), 0)),
            pl.BlockSpec((SUBLANE, QK_PAD), lambda b, pt: (prev(b), 0)),
            pl.BlockSpec((SUBLANE, MLA_DIM), lambda b, pt: (prev(b), U_MG // MLA_DIM)),
            pl.BlockSpec((KV_LORA, MLA_DIM), lambda b, pt: (0, 0)),
            pl.BlockSpec(memory_space=pl.ANY),
            pl.BlockSpec(memory_space=pl.ANY),
        ],
        out_specs=pl.BlockSpec((SUBLANE, MLA_DIM), lambda b, pt: (prev(b), 0)),
        scratch_shapes=[
            pltpu.VMEM((2, nkeys, KV_LORA), _F32),
            pltpu.VMEM((2, n_pages, ROPE_DIM, LANE), _F32),
            pltpu.VMEM((nkeys, KV_LORA), _BF), pltpu.VMEM((nkeys, KV_LORA), _BF),
            pltpu.VMEM((rows, nkeys), _BF), pltpu.VMEM((rows, nkeys), _BF),
            pltpu.VMEM((rows, LANE), _BF), pltpu.VMEM((rows, LANE), _BF),
            pltpu.VMEM((rows, 1), _F32), pltpu.VMEM((rows, 1), _F32),
            pltpu.SemaphoreType.DMA((2, 2)),
        ],
    )
    return pl.pallas_call(
        functools.partial(_attn_sample_kernel, layer=layer, n_pages=n_pages),
        grid_spec=grid_spec,
        out_shape=jax.ShapeDtypeStruct((t, MLA_DIM), _F32),
        compiler_params=_cparams(("arbitrary",)),
        name="attn_sample",
    )(pt_flat, q, kcat, kcat, u, wuv_all, cache_ckv, cache_kpe_t)


def _ssd_chunk(xa_ref, dt, a_row, mask, tmat, hprev_fn, d_ref, y_ref, xw_ref, lastmat=None):
    da = dt * a_row
    acum = _exact_dot_01(tmat, da)
    atot = acum[CHUNK - 1:CHUNK, :] if lastmat is None else _exact_dot_01(lastmat, da)
    acum_t = acum.T
    dt_t = dt.T
    ea = jnp.exp(acum)
    w = jnp.exp(atot - acum) * dt
    heads_per_group = SSM_HEADS // SSM_GROUPS
    for g in range(SSM_GROUPS):
        bm = xa_ref[:, SSM_DIM + g * SSM_STATE:SSM_DIM + (g + 1) * SSM_STATE].astype(_BF)
        cm = xa_ref[:, SSM_DIM + (SSM_GROUPS + g) * SSM_STATE:SSM_DIM + (SSM_GROUPS + g + 1) * SSM_STATE].astype(_BF)
        gmat = _dot_nt(cm, bm)
        yo = hprev_fn(g, cm)
        for hh in range(heads_per_group):
            h = g * heads_per_group + hh
            hs = slice(h * SSM_HEAD_DIM, (h + 1) * SSM_HEAD_DIM)
            seg = acum[:, h:h + 1] - acum_t[h:h + 1, :]
            lmat = jnp.exp(jnp.where(mask, seg, -jnp.inf))
            sc = gmat * (lmat * dt_t[h:h + 1, :])
            xh = xa_ref[:, hs]
            yd = _dot(sc.astype(_BF), xh.astype(_BF))
            yoh = yo[:, hh * SSM_HEAD_DIM:(hh + 1) * SSM_HEAD_DIM] * ea[:, h:h + 1]
            y_ref[:, hs] = yd + yoh + xh * d_ref[:, hs]
            xw_ref[:, hs] = xh * w[:, h:h + 1]
    return acum, atot


def _ssd_gate_norm(y, z, ng_ref, o_ref):
    gated = y * _silu(z)
    gw = SSM_DIM // SSM_GROUPS
    for g in range(SSM_GROUPS):
        cs = slice(g * gw, (g + 1) * gw)
        gg = gated[:, cs]
        ms = jnp.mean(gg * gg, axis=-1, keepdims=True)
        o_ref[:, cs] = (gg * lax.rsqrt(ms + EPS) * ng_ref[:, cs]).astype(o_ref.dtype)


def _short_conv_silu(xp_ref, cw_ref, cb_ref, xa_ref, lead):
    for c in range(XBC_DIM // LANE):
        cs = slice(c * LANE, (c + 1) * LANE)
        acc = jnp.zeros((CHUNK, LANE), _F32) + cb_ref[:, cs]
        for k in range(SSM_CONV):
            off = lead - (SSM_CONV - 1) + k
            acc = acc + cw_ref[k:k + 1, cs] * xp_ref[off:off + CHUNK, cs]
        xa_ref[:, cs] = _silu(acc)


def _ssd_prompt_kernel(xbc_ref, z_ref, dt_ref, cw_ref, cb_ref, dtb_ref, a_ref, d_ref, ng_ref,
                       o_ref, hout_ref, xp_ref, xa_ref, y_ref, xw_ref, h_ref, *, l_valid):
    c = pl.program_id(1)

    @pl.when(c == 0)
    def _():
        xp_ref[0:SUBLANE, :] = jnp.zeros((SUBLANE, XBC_DIM), _F32)
        h_ref[...] = jnp.zeros(h_ref.shape, _F32)

    xp_ref[SUBLANE:SUBLANE + CHUNK, :] = xbc_ref[...]
    _short_conv_silu(xp_ref, cw_ref, cb_ref, xa_ref, SUBLANE)
    xp_ref[0:SUBLANE, :] = xp_ref[CHUNK:CHUNK + SUBLANE, :]

    ri = lax.broadcasted_iota(jnp.int32, (CHUNK, CHUNK), 0)
    ci = lax.broadcasted_iota(jnp.int32, (CHUNK, CHUNK), 1)
    mask = ci <= ri
    tmat = jnp.where(mask, 1.0, 0.0).astype(_BF)
    pos = c * CHUNK + lax.broadcasted_iota(jnp.int32, (CHUNK, LANE), 0)
    dt = jnp.where(pos < l_valid, _softplus(dt_ref[...] + dtb_ref[...]), 0.0)

    hb = h_ref[...].astype(_BF)
    gw = SSM_DIM // SSM_GROUPS

    def hprev_fn(g, cm):
        return _dot_nt(cm, hb[g * gw:(g + 1) * gw, :])

    acum, atot = _ssd_chunk(xa_ref, dt, a_ref[...], mask, tmat, hprev_fn, d_ref, y_ref, xw_ref)
    _ssd_gate_norm(y_ref[...], z_ref[...], ng_ref, o_ref)

    xw_t = xw_ref[...].T
    dec = jnp.exp(atot)
    for g in range(SSM_GROUPS):
        bm = xa_ref[:, SSM_DIM + g * SSM_STATE:SSM_DIM + (g + 1) * SSM_STATE].astype(_BF)
        st = _dot(xw_t[g * gw:(g + 1) * gw, :].astype(_BF), bm)
        for hh in range(SSM_HEADS // SSM_GROUPS):
            h = g * (SSM_HEADS // SSM_GROUPS) + hh
            hs = slice(h * SSM_HEAD_DIM, (h + 1) * SSM_HEAD_DIM)
            h_ref[hs, :] = h_ref[hs, :] * dec[:, h:h + 1] + st[hh * SSM_HEAD_DIM:(hh + 1) * SSM_HEAD_DIM, :]

    @pl.when(c == pl.num_programs(1) - 1)
    def _():
        hout_ref[...] = h_ref[...]


def _ssd_prompt(u, lw, nb, lp, l_valid):
    t = u.shape[0]
    nc = lp // CHUNK
    row = lambda b, i: b * nc + i
    full = lambda shape: pl.BlockSpec(shape, lambda b, i: (0,) * len(shape))
    return pl.pallas_call(
        functools.partial(_ssd_prompt_kernel, l_valid=l_valid),
        grid=(nb, nc),
        in_specs=[
            pl.BlockSpec((CHUNK, XBC_DIM), lambda b, i: (row(b, i), U_XBC // XBC_DIM)),
            pl.BlockSpec((CHUNK, SSM_DIM), lambda b, i: (row(b, i), U_Z // SSM_DIM)),
            pl.BlockSpec((CHUNK, LANE), lambda b, i: (row(b, i), U_DT // LANE)),
            full((SSM_CONV, XBC_DIM)), full((1, XBC_DIM)), full((1, LANE)), full((1, LANE)),
            full((1, SSM_DIM)), full((1, SSM_DIM)),
        ],
        out_specs=[
            pl.BlockSpec((CHUNK, SSM_DIM), lambda b, i: (row(b, i), 0)),
            pl.BlockSpec((None, SSM_DIM, SSM_STATE), lambda b, i: (b, 0, 0)),
        ],
        out_shape=[jax.ShapeDtypeStruct((t, SSM_DIM), _BF), jax.ShapeDtypeStruct((nb, SSM_DIM, SSM_STATE), _F32)],
        scratch_shapes=[
            pltpu.VMEM((CHUNK + SUBLANE, XBC_DIM), _F32), pltpu.VMEM((CHUNK, XBC_DIM), _F32),
            pltpu.VMEM((CHUNK, SSM_DIM), _F32), pltpu.VMEM((CHUNK, SSM_DIM), _F32),
            pltpu.VMEM((SSM_DIM, SSM_STATE), _F32),
        ],
        compiler_params=_cparams(("parallel", "arbitrary")),
        name="ssd_prompt",
    )(u, u, u, lw["scw"], lw["scb"], lw["dtb"], lw["a"], lw["dskip"], lw["sng"])


def _ssd_sample_kernel(xbc_ref, z_ref, dt_ref, left_ref, h0_ref, cw_ref, cb_ref, dtb_ref, a_ref, d_ref, ng_ref,
                       o_ref, hout_ref, xp_ref, xa_ref, y_ref, xw_ref, yo_ref):
    xp_ref[:, 0:SUBLANE, :] = left_ref[...]
    xp_ref[:, SUBLANE:2 * SUBLANE, :] = xbc_ref[...].reshape(REQ_BLK, SUBLANE, XBC_DIM)
    for c in range(XBC_DIM // LANE):
        cs = slice(c * LANE, (c + 1) * LANE)
        acc = jnp.zeros((REQ_BLK, SUBLANE, LANE), _F32) + cb_ref[:, cs][None]
        for k in range(SSM_CONV):
            off = SUBLANE - (SSM_CONV - 1) + k
            acc = acc + cw_ref[k:k + 1, cs][None] * xp_ref[:, off:off + SUBLANE, cs]
        xa_ref[:, cs] = _silu(acc).reshape(CHUNK, LANE)

    ri = lax.broadcasted_iota(jnp.int32, (CHUNK, CHUNK), 0)
    ci = lax.broadcasted_iota(jnp.int32, (CHUNK, CHUNK), 1)
    same = (ri >> 3) == (ci >> 3)
    mask = jnp.logical_and(same, ci <= ri)
    tmat = jnp.where(mask, 1.0, 0.0).astype(_BF)
    lastmat = jnp.where(same, 1.0, 0.0).astype(_BF)
    dt = _softplus(dt_ref[...] + dtb_ref[...])
    gw = SSM_DIM // SSM_GROUPS

    for r in range(REQ_BLK):
        rs = slice(r * SUBLANE, (r + 1) * SUBLANE)
        for g in range(SSM_GROUPS):
            cm = xa_ref[rs, SSM_DIM + (SSM_GROUPS + g) * SSM_STATE:SSM_DIM + (SSM_GROUPS + g + 1) * SSM_STATE]
            hb = h0_ref[r, g * gw:(g + 1) * gw, :].astype(_BF)
            yo_ref[rs, g * gw:(g + 1) * gw] = _dot_nt(cm.astype(_BF), hb)

    def hprev_fn(g, cm):
        return yo_ref[:, g * gw:(g + 1) * gw]

    acum, atot = _ssd_chunk(xa_ref, dt, a_ref[...], mask, tmat, hprev_fn, d_ref, y_ref, xw_ref, lastmat=lastmat)
    _ssd_gate_norm(y_ref[...], z_ref[...], ng_ref, o_ref)

    xw_t = xw_ref[...].T
    dec = jnp.exp(atot)
    lane_req = lax.broadcasted_iota(jnp.int32, (gw, CHUNK), 1) >> 3
    for g in range(SSM_GROUPS):
        bm = xa_ref[:, SSM_DIM + g * SSM_STATE:SSM_DIM + (g + 1) * SSM_STATE].astype(_BF)
        xg = xw_t[g * gw:(g + 1) * gw, :]
        for r in range(REQ_BLK):
            st = _dot(jnp.where(lane_req == r, xg, 0.0).astype(_BF), bm)
            for hh in range(SSM_HEADS // SSM_GROUPS):
                h = g * (SSM_HEADS // SSM_GROUPS) + hh
                hs = slice(h * SSM_HEAD_DIM, (h + 1) * SSM_HEAD_DIM)
                d = dec[r * SUBLANE:r * SUBLANE + 1, h:h + 1]
                hout_ref[r, hs, :] = h0_ref[r, hs, :] * d + st[hh * SSM_HEAD_DIM:(hh + 1) * SSM_HEAD_DIM, :]


def _ssd_sample(u, left8, h0_all, layer, lw):
    t = u.shape[0]
    nreq = t // SUBLANE
    full = lambda shape: pl.BlockSpec(shape, lambda i: (0,) * len(shape))
    return pl.pallas_call(
        _ssd_sample_kernel,
        grid=(nreq // REQ_BLK,),
        in_specs=[
            pl.BlockSpec((CHUNK, XBC_DIM), lambda i: (i, U_XBC // XBC_DIM)),
            pl.BlockSpec((CHUNK, SSM_DIM), lambda i: (i, U_Z // SSM_DIM)),
            pl.BlockSpec((CHUNK, LANE), lambda i: (i, U_DT // LANE)),
            pl.BlockSpec((REQ_BLK, SUBLANE, XBC_DIM), lambda i: (i, 0, 0)),
            pl.BlockSpec((None, REQ_BLK, SSM_DIM, SSM_STATE), lambda i: (layer, i, 0, 0)),
            full((SSM_CONV, XBC_DIM)), full((1, XBC_DIM)), full((1, LANE)), full((1, LANE)),
            full((1, SSM_DIM)), full((1, SSM_DIM)),
        ],
        out_specs=[
            pl.BlockSpec((CHUNK, SSM_DIM), lambda i: (i, 0)),
            pl.BlockSpec((REQ_BLK, SSM_DIM, SSM_STATE), lambda i: (i, 0, 0)),
        ],
        out_shape=[jax.ShapeDtypeStruct((t, SSM_DIM), _BF), jax.ShapeDtypeStruct((nreq, SSM_DIM, SSM_STATE), _F32)],
        scratch_shapes=[
            pltpu.VMEM((REQ_BLK, 2 * SUBLANE, XBC_DIM), _F32), pltpu.VMEM((CHUNK, XBC_DIM), _F32),
            pltpu.VMEM((CHUNK, SSM_DIM), _F32), pltpu.VMEM((CHUNK, SSM_DIM), _F32),
            pltpu.VMEM((CHUNK, SSM_DIM), _F32),
        ],
        compiler_params=_cparams(("parallel",)),
        name="ssd_sample",
    )(u, u, u, left8, h0_all, lw["scw"], lw["scb"], lw["dtb"], lw["a"], lw["dskip"], lw["sng"])


def _outproj_kernel(cy_ref, my_ref, sy_ref, h_ref, w_ref, o_ref, mix_ref):
    @pl.when(pl.program_id(1) == 0)
    def _():
        mix_ref[:, 0:CONV_DIM] = cy_ref[...].astype(_BF)
        mix_ref[:, CONV_DIM:CONV_DIM + MLA_DIM] = my_ref[...].astype(_BF)
        mix_ref[:, CONV_DIM + MLA_DIM:] = sy_ref[...].astype(_BF)

    o_ref[...] = h_ref[...] + _dot(mix_ref[...], w_ref[...])


def _outproj(cy, my, sy, h, w_all, layer, tm, tn=D_MODEL):
    t = h.shape[0]
    return pl.pallas_call(
        _outproj_kernel,
        grid=(t // tm, D_MODEL // tn),
        in_specs=[
            pl.BlockSpec((tm, CONV_DIM), lambda i, j: (i, 0)),
            pl.BlockSpec((tm, MLA_DIM), lambda i, j: (i, 0)),
            pl.BlockSpec((tm, SSM_DIM), lambda i, j: (i, 0)),
            pl.BlockSpec((tm, tn), lambda i, j: (i, j)),
            pl.BlockSpec((None, D_MODEL, tn), lambda i, j: (layer, 0, j)),
        ],
        out_specs=pl.BlockSpec((tm, tn), lambda i, j: (i, j)),
        out_shape=jax.ShapeDtypeStruct((t, D_MODEL), _F32),
        scratch_shapes=[pltpu.VMEM((tm, D_MODEL), _BF)],
        compiler_params=_cparams(("parallel", "arbitrary")),
        name="outproj",
    )(cy, my, sy, h, w_all)


def _final_norm_kernel(x_ref, g_ref, o_ref):
    x = x_ref[...]
    ms = jnp.mean(x * x, axis=-1, keepdims=True)
    o_ref[...] = x * lax.rsqrt(ms + EPS) * g_ref[...]


def _final_norm(h, g, tm):
    t = h.shape[0]
    return pl.pallas_call(
        _final_norm_kernel,
        grid=(t // tm,),
        in_specs=[pl.BlockSpec((tm, D_MODEL), lambda i: (i, 0)), pl.BlockSpec((1, D_MODEL), lambda i: (0, 0))],
        out_specs=pl.BlockSpec((tm, D_MODEL), lambda i: (i, 0)),
        out_shape=jax.ShapeDtypeStruct((t, D_MODEL), _F32),
        compiler_params=_cparams(("parallel",)),
        name="final_norm",
    )(h, g)


def _rot_half_cols(w):
    half = ROPE_DIM // 2
    return jnp.concatenate([-w[..., half:], w[..., :half]], axis=-1)


def _prep_weights(norm_g, w_in, conv_dw_w, conv_dw_b, conv_ln_g, conv_ln_b, conv_pw_w, conv_pw_b, mla_qa_g,
                  mla_wq_b, mla_kva_g, mla_wkv_b, ssm_conv_w, ssm_conv_b, ssm_dt_bias, ssm_A_log, ssm_D,
                  ssm_norm_g, w_out):
    w_in_t = jnp.swapaxes(w_in, 1, 2)
    o = 0
    parts = {}
    for name, width in (("glu", 2 * CONV_DIM), ("cg", CONV_DIM), ("cq", Q_LORA), ("ckv", KV_LORA),
                        ("kpe", ROPE_DIM), ("mg", MLA_DIM), ("z", SSM_DIM), ("xbc", XBC_DIM), ("dt", SSM_HEADS)):
        parts[name] = w_in_t[:, o:o + width, :]
        o += width
    half = ROPE_DIM // 2
    kpe_rot = jnp.concatenate([-parts["kpe"][:, half:], parts["kpe"][:, :half]], axis=1)
    dt_pad = jnp.pad(parts["dt"], ((0, 0), (0, LANE - SSM_HEADS), (0, 0)))
    w_in_p = jnp.concatenate(
        [parts["glu"], parts["mg"], parts["xbc"], parts["cg"], parts["cq"], parts["z"], parts["ckv"],
         parts["kpe"], kpe_rot, dt_pad], axis=1).astype(_BF)

    depth = w_in.shape[0]
    wq_nope = mla_wq_b[..., :NOPE_DIM].reshape(depth, Q_LORA, MLA_HEADS * NOPE_DIM)
    wq_rope = mla_wq_b[..., NOPE_DIM:]
    wq_slots = jnp.concatenate([wq_rope, _rot_half_cols(wq_rope)], axis=-1).reshape(depth, Q_LORA, MLA_HEADS * LANE)
    wq = jnp.concatenate([wq_nope, wq_slots], axis=-1).astype(_BF)
    wuk = jnp.transpose(mla_wkv_b[..., :NOPE_DIM], (0, 2, 3, 1)).astype(_BF)
    wqt = jnp.swapaxes(wq, 1, 2)
    wukt = jnp.transpose(mla_wkv_b[..., :NOPE_DIM], (0, 2, 1, 3)).astype(_BF)
    wuvt = jnp.transpose(mla_wkv_b[..., NOPE_DIM:], (0, 2, 3, 1)).astype(_BF)
    wuv_all = mla_wkv_b[..., NOPE_DIM:].reshape(depth, KV_LORA, MLA_DIM).astype(_BF)

    def row(x):
        return x[:, None, :].astype(_F32)

    pad_heads = lambda x: jnp.pad(x, ((0, 0), (0, LANE - SSM_HEADS)))
    layers = []
    for l in range(depth):
        layers.append({
            "norm_g": row(norm_g)[l],
            "dww": conv_dw_w[l], "dwb": row(conv_dw_b)[l], "lng": row(conv_ln_g)[l], "lnb": row(conv_ln_b)[l],
            "pww": conv_pw_w[l].astype(_BF), "pwb": row(conv_pw_b)[l],
            "qag": row(mla_qa_g)[l], "wq": wq[l], "wuk": wuk[l], "kvg": row(mla_kva_g)[l],
            "wqt": wqt[l], "wukt": wukt[l], "wuvt": wuvt[l], "wuv_all": wuv_all[l],
            "scw": ssm_conv_w[l], "scb": row(ssm_conv_b)[l],
            "dtb": row(pad_heads(ssm_dt_bias))[l], "a": row(pad_heads(-jnp.exp(ssm_A_log.astype(_F32))))[l],
            "dskip": row(jnp.repeat(ssm_D, SSM_HEAD_DIM, axis=-1))[l], "sng": row(ssm_norm_g)[l],
        })
    return layers, w_in_p, w_out.astype(_BF)


def _rope_table(pos):
    inv = ROPE_THETA ** (-jnp.arange(0, ROPE_DIM, 2, dtype=_F32) / ROPE_DIM)
    ang = pos.astype(_F32)[:, None] * inv[None, :]
    c, s = jnp.cos(ang), jnp.sin(ang)
    return jnp.concatenate([c, c, s, s], axis=-1)


def _largest_tile(t, cands):
    for c in cands:
        if t % c == 0:
            return c
    raise ValueError(f"no row tile for {t}")


def kernel(x_prompt, x_sample, cache_ckv, cache_kpe, page_table, state_conv, state_ssm_conv, state_ssm, meta_tokens, norm_g, w_in, conv_dw_w, conv_dw_b, conv_ln_g, conv_ln_b, conv_pw_w, conv_pw_b, mla_qa_g, mla_wq_b, mla_kva_g, mla_wkv_b, ssm_conv_w, ssm_conv_b, ssm_dt_bias, ssm_A_log, ssm_D, ssm_norm_g, w_out, final_g):
    nb, seq, _ = x_prompt.shape
    nreq, dseq, _ = x_sample.shape
    depth = w_in.shape[0]
    n_pages = page_table.shape[1]
    page_size = cache_ckv.shape[2]
    assert dseq == SUBLANE and nreq % REQ_BLK == 0 and page_size == LANE
    l_valid = N_META + seq
    lp = -(-l_valid // ATT_BLK) * ATT_BLK
    tp, ts = nb * lp, nreq * dseq

    layers, w_in_all, w_out_all = _prep_weights(
        norm_g, w_in, conv_dw_w, conv_dw_b, conv_ln_g, conv_ln_b, conv_pw_w, conv_pw_b, mla_qa_g, mla_wq_b,
        mla_kva_g, mla_wkv_b, ssm_conv_w, ssm_conv_b, ssm_dt_bias, ssm_A_log, ssm_D, ssm_norm_g, w_out)

    meta = jnp.broadcast_to(meta_tokens[None].astype(_F32), (nb, N_META, D_MODEL))
    hp = jnp.concatenate([meta, x_prompt, jnp.zeros((nb, lp - l_valid, D_MODEL), _F32)], axis=1).reshape(tp, D_MODEL)
    hs = x_sample.reshape(ts, D_MODEL)
    cs_p = jnp.tile(_rope_table(jnp.arange(lp, dtype=jnp.int32)), (nb, 1))
    cst_p = cs_p.T
    past_len = n_pages * page_size
    cs_s = jnp.tile(_rope_table(past_len + jnp.arange(dseq, dtype=jnp.int32)), (nreq, 1))
    pt_flat = page_table.reshape(-1).astype(jnp.int32)
    cache_kpe_t = jnp.swapaxes(cache_kpe, 2, 3)
    left_conv = jnp.pad(state_conv, ((0, 0), (0, 0), (2, 0), (0, 0)))
    left_ssm = jnp.pad(state_ssm_conv, ((0, 0), (0, 0), (SUBLANE - (SSM_CONV - 1), 0), (0, 0)))
    h0_all = state_ssm.reshape(depth, nreq, SSM_DIM, SSM_STATE)

    tm_p = _largest_tile(tp, (768, 512, 384, 256, 128))
    tm_s = _largest_tile(ts, (512, 256, 128))
    tq_p = _largest_tile(tp, (384, 256, 128))
    tq_s = _largest_tile(ts, (256, 128))

    st_p = [[] for _ in range(5)]
    st_s = [[] for _ in range(5)]
    for l in range(depth):
        lw = layers[l]
        u = _inproj(hp, lw["norm_g"], w_in_all, l, tm_p)
        v, cy = _conv_prompt(u, lw, nb, lp)
        qt, kcat, vt, ckv, kpe = _qkt(u, cs_p, cst_p, lw, nb, lp)
        my = _attn_prompt(qt, kcat, vt, u, lw["wuvt"], nb, lp)
        sy, hfin = _ssd_prompt(u, lw, nb, lp, l_valid)
        hp = _outproj(cy, my, sy, hp, w_out_all, l, tm_p)
        u3 = u.reshape(nb, lp, U_TOTAL)
        st_p[0].append(ckv.reshape(nb, lp, KV_LORA)[:, :l_valid])
        st_p[1].append(kpe.reshape(nb, lp, ROPE_DIM)[:, :l_valid])
        st_p[2].append(v.reshape(nb, lp, CONV_DIM)[:, l_valid - (CONV_WIDTH - 1):l_valid])
        st_p[3].append(u3[:, l_valid - (SSM_CONV - 1):l_valid, U_XBC:U_XBC + XBC_DIM])
        st_p[4].append(hfin.reshape(nb, SSM_HEADS, SSM_HEAD_DIM, SSM_STATE))
        u = _inproj(hs, lw["norm_g"], w_in_all, l, tm_s)
        cst, cy = _conv_sample(u, left_conv[l], lw)
        q, kcat, ckv, kpe = _qk(u, cs_s, lw, tq_s, _F32)
        my = _attn_sample(q, kcat, u, lw["wuv_all"], cache_ckv, cache_kpe_t, pt_flat, l, n_pages)
        sy, hnew = _ssd_sample(u, left_ssm[l], h0_all, l, lw)
        hs = _outproj(cy, my, sy, hs, w_out_all, l, tm_s)
        u3 = u.reshape(nreq, dseq, U_TOTAL)
        st_s[0].append(ckv.reshape(nreq, dseq, KV_LORA))
        st_s[1].append(kpe.reshape(nreq, dseq, ROPE_DIM))
        st_s[2].append(cst[:, 2:, :])
        st_s[3].append(u3[:, dseq - (SSM_CONV - 1):, U_XBC:U_XBC + XBC_DIM])
        st_s[4].append(hnew.reshape(nreq, SSM_HEADS, SSM_HEAD_DIM, SSM_STATE))

    y_prompt = _final_norm(hp, final_g[None, :].astype(_F32), tm_p).reshape(nb, lp, D_MODEL)[:, N_META:l_valid]
    y_sample = _final_norm(hs, final_g[None, :].astype(_F32), tm_s).reshape(nreq, dseq, D_MODEL)
    outs_p = [jnp.stack(s) for s in st_p]
    outs_s = [jnp.stack(s) for s in st_s]
    return (y_prompt, y_sample, *outs_p, *outs_s)
```

```python
import functools
import math

import jax
import jax.numpy as jnp
from jax import lax
from jax.experimental import pallas as pl
from jax.experimental.pallas import tpu as pltpu

D_MODEL = 2048
DEPTH = 4
N_META = 16
CONV_DIM = 512
CONV_WIDTH = 31
MLA_HEADS = 8
MLA_V_DIM = 128
MLA_DIM = MLA_HEADS * MLA_V_DIM
Q_LORA = 512
KV_LORA = 256
NOPE_DIM = 128
ROPE_DIM = 64
ROPE_THETA = 10000.0
MLA_SCALE = 1.0 / math.sqrt(NOPE_DIM + ROPE_DIM)
SSM_DIM = 512
SSM_HEAD_DIM = 64
SSM_HEADS = 8
SSM_GROUPS = 2
SSM_STATE = 128
SSM_CONV = 4
XBC_DIM = SSM_DIM + 2 * SSM_GROUPS * SSM_STATE
EPS = 1e-6

LANE = 128
SUBLANE = 8
VMEM_LIMIT = 48 * 1024 * 1024

U_GLU = 0
U_MG = 1024
U_XBC = 2048
U_CG = 3072
U_CQ = 3584
U_Z = 4096
U_CKV = 4608
U_KPE = 4864
U_DT = 4992
U_TOTAL = 5120

QK_PAD = 384
CHUNK = 128
ATT_BLK = 256
ATT_GROUP = 4
EXP2_SCALE = MLA_SCALE * math.log2(math.e)
_BF = jnp.bfloat16
_F32 = jnp.float32


def _cparams(sem):
    return pltpu.CompilerParams(dimension_semantics=sem, vmem_limit_bytes=VMEM_LIMIT)


def _dot(a, b):
    return jnp.dot(a, b, preferred_element_type=_F32)


def _dot_nt(a, b):
    return lax.dot_general(a, b, (((1,), (1,)), ((), ())), preferred_element_type=_F32)


def _sigmoid(x):
    return 1.0 / (1.0 + jnp.exp(-x))


def _silu(x):
    return x * _sigmoid(x)


def _softplus(x):
    return jnp.maximum(x, 0.0) + jnp.log(1.0 + jnp.exp(-jnp.abs(x)))


def _exact_dot_01(mat01_bf16, x):
    hi = x.astype(_BF)
    r1 = x - hi.astype(_F32)
    mid = r1.astype(_BF)
    lo = (r1 - mid.astype(_F32)).astype(_BF)
    return _dot(mat01_bf16, hi) + _dot(mat01_bf16, mid) + _dot(mat01_bf16, lo)


def _inproj_kernel(x_ref, g_ref, w_ref, o_ref, xn_ref):
    @pl.when(pl.program_id(1) == 0)
    def _():
        x = x_ref[...]
        ms = jnp.mean(x * x, axis=-1, keepdims=True)
        xn_ref[...] = (x * lax.rsqrt(ms + EPS) * g_ref[...]).astype(_BF)

    o_ref[...] = _dot_nt(xn_ref[...], w_ref[...])


def _inproj(h, g, w_all, layer, tm, tn=U_TOTAL // 2):
    t = h.shape[0]
    n = w_all.shape[1]
    return pl.pallas_call(
        _inproj_kernel,
        grid=(t // tm, n // tn),
        in_specs=[
            pl.BlockSpec((tm, D_MODEL), lambda i, j: (i, 0)),
            pl.BlockSpec((1, D_MODEL), lambda i, j: (0, 0)),
            pl.BlockSpec((None, tn, D_MODEL), lambda i, j: (layer, j, 0)),
        ],
        out_specs=pl.BlockSpec((tm, tn), lambda i, j: (i, j)),
        out_shape=jax.ShapeDtypeStruct((t, n), _F32),
        scratch_shapes=[pltpu.VMEM((tm, D_MODEL), _BF)],
        compiler_params=_cparams(("parallel", "arbitrary")),
        name="inproj",
    )(h, g, w_all)


def _conv_tail(y, cg, lng_ref, lnb_ref, pww_ref, pwb_ref):
    mu = jnp.mean(y, axis=-1, keepdims=True)
    yc = y - mu
    var = jnp.mean(yc * yc, axis=-1, keepdims=True)
    yn = yc * lax.rsqrt(var + EPS) * lng_ref[...] + lnb_ref[...]
    s = _silu(yn)
    o = _dot(s.astype(_BF), pww_ref[...]) + pwb_ref[...]
    return o * _silu(cg)


def _conv_prompt_kernel(glu_ref, cg_ref, dww_ref, dwb_ref, lng_ref, lnb_ref, pww_ref, pwb_ref,
                        v_ref, cy_ref, vp_ref, y_ref, sh_ref):
    @pl.when(pl.program_id(1) == 0)
    def _():
        vp_ref[0:32, :] = jnp.zeros((32, CONV_DIM), _F32)

    v = glu_ref[:, :CONV_DIM] * _sigmoid(glu_ref[:, CONV_DIM:])
    v_ref[...] = v
    vp_ref[32:32 + CHUNK, :] = v
    for c in range(CONV_DIM // LANE):
        cs = slice(c * LANE, (c + 1) * LANE)
        acc = jnp.zeros((CHUNK, LANE), _F32)
        for r in range(SUBLANE):
            taps = range(r, CONV_WIDTH, SUBLANE)
            span = CHUNK + SUBLANE * (len(taps) - 1)
            sh_ref[0:span, :] = vp_ref[2 + r:2 + r + span, cs]
            for j, k in enumerate(taps):
                acc = acc + dww_ref[k:k + 1, cs] * sh_ref[SUBLANE * j:SUBLANE * j + CHUNK, :]
        y_ref[:, cs] = acc + dwb_ref[:, cs]
    cy = _conv_tail(y_ref[...], cg_ref[...], lng_ref, lnb_ref, pww_ref, pwb_ref)
    cy_ref[...] = cy.astype(cy_ref.dtype)
    vp_ref[0:32, :] = vp_ref[CHUNK:CHUNK + 32, :]


def _conv_prompt(u, lw, nb, lp):
    t = u.shape[0]
    nt = lp // CHUNK
    row = lambda b, i: b * nt + i
    full = lambda shape: pl.BlockSpec(shape, lambda b, i: (0,) * len(shape))
    return pl.pallas_call(
        _conv_prompt_kernel,
        grid=(nb, nt),
        in_specs=[
            pl.BlockSpec((CHUNK, 2 * CONV_DIM), lambda b, i: (row(b, i), U_GLU // (2 * CONV_DIM))),
            pl.BlockSpec((CHUNK, CONV_DIM), lambda b, i: (row(b, i), U_CG // CONV_DIM)),
            full((CONV_WIDTH, CONV_DIM)), full((1, CONV_DIM)), full((1, CONV_DIM)), full((1, CONV_DIM)),
            full((CONV_DIM, CONV_DIM)), full((1, CONV_DIM)),
        ],
        out_specs=[
            pl.BlockSpec((CHUNK, CONV_DIM), lambda b, i: (row(b, i), 0)),
            pl.BlockSpec((CHUNK, CONV_DIM), lambda b, i: (row(b, i), 0)),
        ],
        out_shape=[jax.ShapeDtypeStruct((t, CONV_DIM), _F32), jax.ShapeDtypeStruct((t, CONV_DIM), _BF)],
        scratch_shapes=[pltpu.VMEM((CHUNK + 32, CONV_DIM), _F32), pltpu.VMEM((CHUNK, CONV_DIM), _F32),
                        pltpu.VMEM((CHUNK + 32, LANE), _F32)],
        compiler_params=_cparams(("parallel", "arbitrary")),
        name="conv_prompt",
    )(u, u, lw["dww"], lw["dwb"], lw["lng"], lw["lnb"], lw["pww"], lw["pwb"])


REQ_BLK = CHUNK // SUBLANE


def _conv_sample_kernel(glu_ref, cg_ref, left_ref, dww_ref, dwb_ref, lng_ref, lnb_ref, pww_ref, pwb_ref,
                        st_ref, cy_ref, vp_ref, y_ref):
    v = glu_ref[:, :CONV_DIM] * _sigmoid(glu_ref[:, CONV_DIM:])
    vp_ref[:, 0:32, :] = left_ref[...]
    vp_ref[:, 32:40, :] = v.reshape(REQ_BLK, SUBLANE, CONV_DIM)
    for c in range(CONV_DIM // LANE):
        cs = slice(c * LANE, (c + 1) * LANE)
        acc = jnp.zeros((REQ_BLK, SUBLANE, LANE), _F32)
        for k in range(CONV_WIDTH):
            acc = acc + dww_ref[k:k + 1, cs][None] * vp_ref[:, 2 + k:2 + k + SUBLANE, cs]
        y_ref[:, :, cs] = acc + dwb_ref[:, cs][None]
    y = y_ref[...].reshape(CHUNK, CONV_DIM)
    cy = _conv_tail(y, cg_ref[...], lng_ref, lnb_ref, pww_ref, pwb_ref)
    cy_ref[...] = cy.astype(cy_ref.dtype)
    st_ref[...] = vp_ref[:, 8:40, :]


def _conv_sample(u, left32, lw):
    t = u.shape[0]
    nreq = t // SUBLANE
    full = lambda shape: pl.BlockSpec(shape, lambda i: (0,) * len(shape))
    return pl.pallas_call(
        _conv_sample_kernel,
        grid=(nreq // REQ_BLK,),
        in_specs=[
            pl.BlockSpec((CHUNK, 2 * CONV_DIM), lambda i: (i, U_GLU // (2 * CONV_DIM))),
            pl.BlockSpec((CHUNK, CONV_DIM), lambda i: (i, U_CG // CONV_DIM)),
            pl.BlockSpec((REQ_BLK, 32, CONV_DIM), lambda i: (i, 0, 0)),
            full((CONV_WIDTH, CONV_DIM)), full((1, CONV_DIM)), full((1, CONV_DIM)), full((1, CONV_DIM)),
            full((CONV_DIM, CONV_DIM)), full((1, CONV_DIM)),
        ],
        out_specs=[
            pl.BlockSpec((REQ_BLK, 32, CONV_DIM), lambda i: (i, 0, 0)),
            pl.BlockSpec((CHUNK, CONV_DIM), lambda i: (i, 0)),
        ],
        out_shape=[jax.ShapeDtypeStruct((nreq, 32, CONV_DIM), _F32), jax.ShapeDtypeStruct((t, CONV_DIM), _BF)],
        scratch_shapes=[pltpu.VMEM((REQ_BLK, 40, CONV_DIM), _F32), pltpu.VMEM((REQ_BLK, SUBLANE, CONV_DIM), _F32)],
        compiler_params=_cparams(("parallel",)),
        name="conv_sample",
    )(u, u, left32, lw["dww"], lw["dwb"], lw["lng"], lw["lnb"], lw["pww"], lw["pwb"])


def _rope_slot(slot, cs):
    r = slot * cs
    r = r + pltpu.roll(r, 64, axis=1)
    lane = lax.broadcasted_iota(jnp.int32, r.shape, 1)
    return jnp.where(lane < ROPE_DIM, r, 0.0)


def _qk_kernel(cq_ref, ckv_ref, kpe_ref, cs_ref, qag_ref, wq_ref, wuk_ref, kvg_ref,
               q_ref, kcat_ref, ckvo_ref, kpeo_ref):
    cq = cq_ref[...]
    ms = jnp.mean(cq * cq, axis=-1, keepdims=True)
    cqn = (cq * lax.rsqrt(ms + EPS) * qag_ref[...]).astype(_BF)
    q = _dot(cqn, wq_ref[...])
    cs = cs_ref[...]
    for h in range(MLA_HEADS):
        qn = q[:, h * NOPE_DIM:(h + 1) * NOPE_DIM].astype(_BF)
        ql = _dot(qn, wuk_ref[h])
        q_ref[h, :, 0:KV_LORA] = ql.astype(q_ref.dtype)
        slot = q[:, MLA_HEADS * NOPE_DIM + h * LANE:MLA_HEADS * NOPE_DIM + (h + 1) * LANE]
        q_ref[h, :, KV_LORA:QK_PAD] = _rope_slot(slot, cs).astype(q_ref.dtype)
    c = ckv_ref[...]
    ms = jnp.mean(c * c, axis=-1, keepdims=True)
    ckv = c * lax.rsqrt(ms + EPS) * kvg_ref[...]
    ckvo_ref[...] = ckv
    kcat_ref[:, 0:KV_LORA] = ckv.astype(kcat_ref.dtype)
    kr = _rope_slot(kpe_ref[...], cs)
    kcat_ref[:, KV_LORA:QK_PAD] = kr.astype(kcat_ref.dtype)
    kpeo_ref[...] = kr[:, :ROPE_DIM]


def _qk(u, cs, lw, tm, qdtype):
    t = u.shape[0]
    full = lambda shape: pl.BlockSpec(shape, lambda i: (0,) * len(shape))
    return pl.pallas_call(
        _qk_kernel,
        grid=(t // tm,),
        in_specs=[
            pl.BlockSpec((tm, Q_LORA), lambda i: (i, U_CQ // Q_LORA)),
            pl.BlockSpec((tm, KV_LORA), lambda i: (i, U_CKV // KV_LORA)),
            pl.BlockSpec((tm, LANE), lambda i: (i, U_KPE // LANE)),
            pl.BlockSpec((tm, LANE), lambda i: (i, 0)),
            full((1, Q_LORA)), full((Q_LORA, 2 * MLA_HEADS * NOPE_DIM)),
            full((MLA_HEADS, NOPE_DIM, KV_LORA)), full((1, KV_LORA)),
        ],
        out_specs=[
            pl.BlockSpec((MLA_HEADS, tm, QK_PAD), lambda i: (0, i, 0)),
            pl.BlockSpec((tm, QK_PAD), lambda i: (i, 0)),
            pl.BlockSpec((tm, KV_LORA), lambda i: (i, 0)),
            pl.BlockSpec((tm, ROPE_DIM), lambda i: (i, 0)),
        ],
        out_shape=[
            jax.ShapeDtypeStruct((MLA_HEADS, t, QK_PAD), qdtype),
            jax.ShapeDtypeStruct((t, QK_PAD), qdtype),
            jax.ShapeDtypeStruct((t, KV_LORA), _F32),
            jax.ShapeDtypeStruct((t, ROPE_DIM), _F32),
        ],
        compiler_params=_cparams(("parallel",)),
        name="qk",
    )(u, u, u, cs, lw["qag"], lw["wq"], lw["wuk"], lw["kvg"])


def _qkt_kernel(cq_ref, ckv_ref, kpe_ref, cs_ref, cst_ref, qag_ref, wqt_ref, wukt_ref, kvg_ref,
                qt_ref, kcat_ref, vt_ref, ckvo_ref, kpeo_ref):
    tm = ATT_BLK
    cq = cq_ref[...]
    ms = jnp.mean(cq * cq, axis=-1, keepdims=True)
    cqn_t = (cq * lax.rsqrt(ms + EPS) * qag_ref[...]).T.astype(_BF)
    qt = _dot(wqt_ref[...], cqn_t)
    cst = cst_ref[...]
    for h in range(MLA_HEADS):
        cols = slice(h * tm, (h + 1) * tm)
        qn_t = qt[h * NOPE_DIM:(h + 1) * NOPE_DIM].astype(_BF)
        qt_ref[0:KV_LORA, cols] = _dot(wukt_ref[h], qn_t).astype(qt_ref.dtype)
        r = qt[MLA_HEADS * NOPE_DIM + h * LANE:MLA_HEADS * NOPE_DIM + (h + 1) * LANE] * cst
        qt_ref[KV_LORA:KV_LORA + ROPE_DIM, cols] = (r[:ROPE_DIM] + r[ROPE_DIM:]).astype(qt_ref.dtype)
        qt_ref[KV_LORA + ROPE_DIM:QK_PAD, cols] = jnp.zeros((QK_PAD - KV_LORA - ROPE_DIM, tm), qt_ref.dtype)
    c = ckv_ref[...]
    ms = jnp.mean(c * c, axis=-1, keepdims=True)
    ckv = c * lax.rsqrt(ms + EPS) * kvg_ref[...]
    ckvo_ref[...] = ckv
    kcat_ref[:, 0:KV_LORA] = ckv.astype(kcat_ref.dtype)
    vt_ref[...] = ckv.T.astype(vt_ref.dtype)
    kr = _rope_slot(kpe_ref[...], cs_ref[...])
    kcat_ref[:, KV_LORA:QK_PAD] = kr.astype(kcat_ref.dtype)
    kpeo_ref[...] = kr[:, :ROPE_DIM]


def _qkt(u, cs, cst, lw, nb, lp):
    t = u.shape[0]
    tm = ATT_BLK
    nblk = t // tm
    full = lambda shape: pl.BlockSpec(shape, lambda i: (0,) * len(shape))
    return pl.pallas_call(
        _qkt_kernel,
        grid=(nblk,),
        in_specs=[
            pl.BlockSpec((tm, Q_LORA), lambda i: (i, U_CQ // Q_LORA)),
            pl.BlockSpec((tm, KV_LORA), lambda i: (i, U_CKV // KV_LORA)),
            pl.BlockSpec((tm, LANE), lambda i: (i, U_KPE // LANE)),
            pl.BlockSpec((tm, LANE), lambda i: (i, 0)),
            pl.BlockSpec((LANE, tm), lambda i: (0, i)),
            full((1, Q_LORA)), full((2 * MLA_HEADS * NOPE_DIM, Q_LORA)),
            full((MLA_HEADS, KV_LORA, NOPE_DIM)), full((1, KV_LORA)),
        ],
        out_specs=[
            pl.BlockSpec((None, QK_PAD, MLA_HEADS * tm), lambda i: (i, 0, 0)),
            pl.BlockSpec((tm, QK_PAD), lambda i: (i, 0)),
            pl.BlockSpec((None, KV_LORA, tm), lambda i: (i, 0, 0)),
            pl.BlockSpec((tm, KV_LORA), lambda i: (i, 0)),
            pl.BlockSpec((tm, ROPE_DIM), lambda i: (i, 0)),
        ],
        out_shape=[
            jax.ShapeDtypeStruct((nblk, QK_PAD, MLA_HEADS * tm), _BF),
            jax.ShapeDtypeStruct((t, QK_PAD), _BF),
            jax.ShapeDtypeStruct((nblk, KV_LORA, tm), _BF),
            jax.ShapeDtypeStruct((t, KV_LORA), _F32),
            jax.ShapeDtypeStruct((t, ROPE_DIM), _F32),
        ],
        compiler_params=_cparams(("parallel",)),
        name="qkt",
    )(u, u, u, cs, cst, lw["qag"], lw["wqt"], lw["wukt"], lw["kvg"])


def _attn_prompt_kernel(qt_ref, k_ref, vt_ref, mg_ref, wuvt_ref, o_ref, m_ref, l_ref, acc_ref):
    qi = pl.program_id(1)
    ncol = MLA_HEADS * ATT_BLK
    kk = lax.broadcasted_iota(jnp.int32, (ATT_BLK, ATT_BLK), 0)
    qq = lax.broadcasted_iota(jnp.int32, (ATT_BLK, ATT_BLK), 1)
    diag_mask = jnp.concatenate([kk <= qq] * MLA_HEADS, axis=1)

    m_ref[...] = jnp.full((1, ncol), -jnp.inf, _F32)
    l_ref[...] = jnp.zeros((1, ncol), _F32)
    acc_ref[...] = jnp.zeros((KV_LORA, ncol), _F32)

    def group(c, n, masked):
        qt = qt_ref[...]
        ss = []
        for j in range(n):
            kc = k_ref[pl.ds(pl.multiple_of((c + j) * ATT_BLK, ATT_BLK), ATT_BLK), :]
            ss.append(_dot(kc, qt))
        if masked:
            ss[-1] = jnp.where(diag_mask, ss[-1], -jnp.inf)
        m_prev = m_ref[...]
        m_new = m_prev
        for s in ss:
            m_new = jnp.maximum(m_new, jnp.max(s, axis=0, keepdims=True))
        alpha = jnp.exp2((m_prev - m_new) * EXP2_SCALE)
        l_new = alpha * l_ref[...]
        pv = None
        for j, s in enumerate(ss):
            p = jnp.exp2((s - m_new) * EXP2_SCALE)
            l_new = l_new + jnp.sum(p, axis=0, keepdims=True)
            d = _dot(vt_ref[c + j], p.astype(_BF))
            pv = d if pv is None else pv + d
        l_ref[...] = l_new
        acc_ref[...] = alpha * acc_ref[...] + pv
        m_ref[...] = m_new

    def body(cg, carry):
        group(cg * ATT_GROUP, ATT_GROUP, False)
        return carry

    ngroups = qi >> (ATT_GROUP.bit_length() - 1)
    lax.fori_loop(0, ngroups, body, 0)
    base = ngroups * ATT_GROUP
    n = ATT_GROUP // 2
    while n >= 1:
        @pl.when((qi & n) != 0)
        def _(base=base, n=n):
            group(base, n, False)
        base = base + (qi & n)
        n //= 2
    group(qi, 1, True)
    for h in range(MLA_HEADS):
        cols = slice(h * ATT_BLK, (h + 1) * ATT_BLK)
        o_t = (acc_ref[:, cols] / l_ref[:, cols]).astype(_BF)
        y = _dot(wuvt_ref[h], o_t).T
        g = mg_ref[:, h * MLA_V_DIM:(h + 1) * MLA_V_DIM]
        o_ref[:, h * MLA_V_DIM:(h + 1) * MLA_V_DIM] = (y * _silu(g)).astype(o_ref.dtype)


def _attn_prompt(qt, kcat, vt, u, wuvt, nb, lp):
    t = kcat.shape[0]
    nq = lp // ATT_BLK
    return pl.pallas_call(
        _attn_prompt_kernel,
        grid=(nb, nq),
        in_specs=[
            pl.BlockSpec((None, QK_PAD, MLA_HEADS * ATT_BLK), lambda b, i: (b * nq + i, 0, 0)),
            pl.BlockSpec((None, lp, QK_PAD), lambda b, i: (b, 0, 0)),
            pl.BlockSpec((None, nq, KV_LORA, ATT_BLK), lambda b, i: (b, 0, 0, 0)),
            pl.BlockSpec((ATT_BLK, MLA_DIM), lambda b, i: (b * nq + i, U_MG // MLA_DIM)),
            pl.BlockSpec((MLA_HEADS, MLA_V_DIM, KV_LORA), lambda b, i: (0, 0, 0)),
        ],
        out_specs=pl.BlockSpec((ATT_BLK, MLA_DIM), lambda b, i: (b * nq + i, 0)),
        out_shape=jax.ShapeDtypeStruct((t, MLA_DIM), _BF),
        scratch_shapes=[
            pltpu.VMEM((1, MLA_HEADS * ATT_BLK), _F32),
            pltpu.VMEM((1, MLA_HEADS * ATT_BLK), _F32),
            pltpu.VMEM((KV_LORA, MLA_HEADS * ATT_BLK), _F32),
        ],
        compiler_params=_cparams(("parallel", "arbitrary")),
        name="attn_prompt",
    )(qt, kcat.reshape(nb, lp, QK_PAD), vt.reshape(nb, nq, KV_LORA, ATT_BLK), u, wuvt)


def _page_copies(pt_ref, ckv_hbm, kpe_hbm, ckv_buf, kpe_buf, sem, req, slot, layer, n_pages):
    copies = []
    for i in range(n_pages):
        pg = pt_ref[req * n_pages + i]
        copies.append(pltpu.make_async_copy(
            ckv_hbm.at[layer, pg], ckv_buf.at[slot, pl.ds(i * LANE, LANE), :], sem.at[0, slot]))
        copies.append(pltpu.make_async_copy(kpe_hbm.at[layer, pg], kpe_buf.at[slot, i], sem.at[1, slot]))
    return copies


def _pad_new_keys(kn_ref):
    return jnp.concatenate([kn_ref[...], jnp.zeros((LANE - SUBLANE, QK_PAD), _F32)], axis=0).astype(_BF)


def _attn_sample_kernel(pt_ref, q_ref, kn_ref, knp_ref, mg_ref, wuv_ref, ckv_hbm, kpe_hbm, o_ref,
                        ckv_buf, kpe_buf, kbf_a, kbf_b, p_a, p_b, pn_a, pn_b, l_a, l_b, sem, *, layer, n_pages):
    b = pl.program_id(0)
    nreq = pl.num_programs(0) - 1
    slot = b & 1
    rows = MLA_HEADS * SUBLANE
    fetch = functools.partial(_page_copies, pt_ref, ckv_hbm, kpe_hbm, ckv_buf, kpe_buf, sem,
                              layer=layer, n_pages=n_pages)

    @pl.when(b == 0)
    def _():
        for c in fetch(req=b, slot=slot):
            c.start()
        kbf_b[...] = jnp.zeros(kbf_b.shape, _BF)
        p_b[...] = jnp.zeros(p_b.shape, _BF)
        pn_b[...] = jnp.zeros(pn_b.shape, _BF)
        l_b[...] = jnp.ones(l_b.shape, _F32)

    last = nreq - 1
    for c in fetch(req=jnp.minimum(b, last), slot=slot):
        c.wait()

    def step(kbf_w, p_w, pn_w, l_w, kbf_r, p_r, pn_r, l_r):
        qb = q_ref[...].reshape(rows, QK_PAD).astype(_BF)
        kbf_w[...] = ckv_buf[slot].astype(_BF)
        kpt = jnp.concatenate([kpe_buf[slot, i] for i in range(n_pages)], axis=1).astype(_BF)
        s = (_dot_nt(qb[:, :KV_LORA], kbf_w[...]) + _dot(qb[:, KV_LORA:KV_LORA + ROPE_DIM], kpt)) * MLA_SCALE
        sn = _dot_nt(qb, _pad_new_keys(kn_ref)) * MLA_SCALE
        for c in fetch(req=jnp.minimum(b + 1, last), slot=1 - slot):
            c.start()
        tq = lax.broadcasted_iota(jnp.int32, sn.shape, 0) & (SUBLANE - 1)
        tk = lax.broadcasted_iota(jnp.int32, sn.shape, 1)
        sn = jnp.where(tk <= tq, sn, -jnp.inf)
        m = jnp.maximum(jnp.max(s, axis=-1, keepdims=True), jnp.max(sn, axis=-1, keepdims=True))
        p = jnp.exp(s - m)
        pn = jnp.exp(sn - m)
        l_w[...] = jnp.sum(p, axis=-1, keepdims=True) + jnp.sum(pn, axis=-1, keepdims=True)
        p_w[...] = p.astype(_BF)
        pn_w[...] = pn.astype(_BF)
        acc = _dot(p_r[...], kbf_r[...]) + _dot(pn_r[...], _pad_new_keys(knp_ref)[:, :KV_LORA])
        o = (acc / l_r[...]).astype(_BF)
        y = _dot(o, wuv_ref[...])
        mg = mg_ref[...]
        for h in range(MLA_HEADS):
            cs = slice(h * MLA_V_DIM, (h + 1) * MLA_V_DIM)
            yh = y[h * SUBLANE:(h + 1) * SUBLANE, cs]
            o_ref[:, cs] = (yh * _silu(mg[:, cs])).astype(o_ref.dtype)

    @pl.when(slot == 0)
    def _():
        step(kbf_a, p_a, pn_a, l_a, kbf_b, p_b, pn_b, l_b)

    @pl.when(slot == 1)
    def _():
        step(kbf_b, p_b, pn_b, l_b, kbf_a, p_a, pn_a, l_a)

    @pl.when(b == nreq)
    def _():
        for c in fetch(req=last, slot=1 - slot):
            c.wait()


def _attn_sample(q, kcat, u, wuv_all, cache_ckv, cache_kpe_t, pt_flat, layer, n_pages):
    t = kcat.shape[0]
    nreq = t // SUBLANE
    nkeys = n_pages * LANE
    rows = MLA_HEADS * SUBLANE
    cur = lambda b: jnp.minimum(b, nreq - 1)
    prev = lambda b: jnp.maximum(b - 1, 0)
    grid_spec = pltpu.PrefetchScalarGridSpec(
        num_scalar_prefetch=1,
        grid=(nreq + 1,),
        in_specs=[
            pl.BlockSpec((MLA_HEADS, SUBLANE, QK_PAD), lambda b, pt: (0, cur(b), 0)),
            pl.BlockSpec((SUBLANE, QK_PAD), lambda b, pt: (cur(b), 0)),
            pl.BlockSpec((SUBLANE, QK_PAD), lambda b, pt: (prev(b), 0)),
            pl.BlockSpec((SUBLANE, MLA_DIM), lambda b, pt: (prev(b), U_MG // MLA_DIM)),
            pl.BlockSpec((KV_LORA, MLA_DIM), lambda b, pt: (0, 0)),
            pl.BlockSpec(memory_space=pl.ANY),
            pl.BlockSpec(memory_space=pl.ANY),
        ],
        out_specs=pl.BlockSpec((SUBLANE, MLA_DIM), lambda b, pt: (prev(b), 0)),
        scratch_shapes=[
            pltpu.VMEM((2, nkeys, KV_LORA), _F32),
            pltpu.VMEM((2, n_pages, ROPE_DIM, LANE), _F32),
            pltpu.VMEM((nkeys, KV_LORA), _BF), pltpu.VMEM((nkeys, KV_LORA), _BF),
            pltpu.VMEM((rows, nkeys), _BF), pltpu.VMEM((rows, nkeys), _BF),
            pltpu.VMEM((rows, LANE), _BF), pltpu.VMEM((rows, LANE), _BF),
            pltpu.VMEM((rows, 1), _F32), pltpu.VMEM((rows, 1), _F32),
            pltpu.SemaphoreType.DMA((2, 2)),
        ],
    )
    return pl.pallas_call(
        functools.partial(_attn_sample_kernel, layer=layer, n_pages=n_pages),
        grid_spec=grid_spec,
        out_shape=jax.ShapeDtypeStruct((t, MLA_DIM), _F32),
        compiler_params=_cparams(("arbitrary",)),
        name="attn_sample",
    )(pt_flat, q, kcat, kcat, u, wuv_all, cache_ckv, cache_kpe_t)


def _ssd_chunk(xa_ref, dt, a_row, mask, tmat, hprev_fn, d_ref, y_ref, xw_ref, lastmat=None):
    da = dt * a_row
    acum = _exact_dot_01(tmat, da)
    atot = acum[CHUNK - 1:CHUNK, :] if lastmat is None else _exact_dot_01(lastmat, da)
    acum_t = acum.T
    dt_t = dt.T
    ea = jnp.exp(acum)
    w = jnp.exp(atot - acum) * dt
    heads_per_group = SSM_HEADS // SSM_GROUPS
    for g in range(SSM_GROUPS):
        bm = xa_ref[:, SSM_DIM + g * SSM_STATE:SSM_DIM + (g + 1) * SSM_STATE].astype(_BF)
        cm = xa_ref[:, SSM_DIM + (SSM_GROUPS + g) * SSM_STATE:SSM_DIM + (SSM_GROUPS + g + 1) * SSM_STATE].astype(_BF)
        gmat = _dot_nt(cm, bm)
        yo = hprev_fn(g, cm)
        for hh in range(heads_per_group):
            h = g * heads_per_group + hh
            hs = slice(h * SSM_HEAD_DIM, (h + 1) * SSM_HEAD_DIM)
            seg = acum[:, h:h + 1] - acum_t[h:h + 1, :]
            lmat = jnp.exp(jnp.where(mask, seg, -jnp.inf))
            sc = gmat * (lmat * dt_t[h:h + 1, :])
            xh = xa_ref[:, hs]
            yd = _dot(sc.astype(_BF), xh.astype(_BF))
            yoh = yo[:, hh * SSM_HEAD_DIM:(hh + 1) * SSM_HEAD_DIM] * ea[:, h:h + 1]
            y_ref[:, hs] = yd + yoh + xh * d_ref[:, hs]
            xw_ref[:, hs] = xh * w[:, h:h + 1]
    return acum, atot


def _ssd_gate_norm(y, z, ng_ref, o_ref):
    gated = y * _silu(z)
    gw = SSM_DIM // SSM_GROUPS
    for g in range(SSM_GROUPS):
        cs = slice(g * gw, (g + 1) * gw)
        gg = gated[:, cs]
        ms = jnp.mean(gg * gg, axis=-1, keepdims=True)
        o_ref[:, cs] = (gg * lax.rsqrt(ms + EPS) * ng_ref[:, cs]).astype(o_ref.dtype)


def _short_conv_silu(xp_ref, cw_ref, cb_ref, xa_ref, lead):
    for c in range(XBC_DIM // LANE):
        cs = slice(c * LANE, (c + 1) * LANE)
        acc = jnp.zeros((CHUNK, LANE), _F32) + cb_ref[:, cs]
        for k in range(SSM_CONV):
            off = lead - (SSM_CONV - 1) + k
            acc = acc + cw_ref[k:k + 1, cs] * xp_ref[off:off + CHUNK, cs]
        xa_ref[:, cs] = _silu(acc)


def _ssd_prompt_kernel(xbc_ref, z_ref, dt_ref, cw_ref, cb_ref, dtb_ref, a_ref, d_ref, ng_ref,
                       o_ref, hout_ref, xp_ref, xa_ref, y_ref, xw_ref, h_ref, *, l_valid):
    c = pl.program_id(1)

    @pl.when(c == 0)
    def _():
        xp_ref[0:SUBLANE, :] = jnp.zeros((SUBLANE, XBC_DIM), _F32)
        h_ref[...] = jnp.zeros(h_ref.shape, _F32)

    xp_ref[SUBLANE:SUBLANE + CHUNK, :] = xbc_ref[...]
    _short_conv_silu(xp_ref, cw_ref, cb_ref, xa_ref, SUBLANE)
    xp_ref[0:SUBLANE, :] = xp_ref[CHUNK:CHUNK + SUBLANE, :]

    ri = lax.broadcasted_iota(jnp.int32, (CHUNK, CHUNK), 0)
    ci = lax.broadcasted_iota(jnp.int32, (CHUNK, CHUNK), 1)
    mask = ci <= ri
    tmat = jnp.where(mask, 1.0, 0.0).astype(_BF)
    pos = c * CHUNK + lax.broadcasted_iota(jnp.int32, (CHUNK, LANE), 0)
    dt = jnp.where(pos < l_valid, _softplus(dt_ref[...] + dtb_ref[...]), 0.0)

    hb = h_ref[...].astype(_BF)
    gw = SSM_DIM // SSM_GROUPS

    def hprev_fn(g, cm):
        return _dot_nt(cm, hb[g * gw:(g + 1) * gw, :])

    acum, atot = _ssd_chunk(xa_ref, dt, a_ref[...], mask, tmat, hprev_fn, d_ref, y_ref, xw_ref)
    _ssd_gate_norm(y_ref[...], z_ref[...], ng_ref, o_ref)

    xw_t = xw_ref[...].T
    dec = jnp.exp(atot)
    for g in range(SSM_GROUPS):
        bm = xa_ref[:, SSM_DIM + g * SSM_STATE:SSM_DIM + (g + 1) * SSM_STATE].astype(_BF)
        st = _dot(xw_t[g * gw:(g + 1) * gw, :].astype(_BF), bm)
        for hh in range(SSM_HEADS // SSM_GROUPS):
            h = g * (SSM_HEADS // SSM_GROUPS) + hh
            hs = slice(h * SSM_HEAD_DIM, (h + 1) * SSM_HEAD_DIM)
            h_ref[hs, :] = h_ref[hs, :] * dec[:, h:h + 1] + st[hh * SSM_HEAD_DIM:(hh + 1) * SSM_HEAD_DIM, :]

    @pl.when(c == pl.num_programs(1) - 1)
    def _():
        hout_ref[...] = h_ref[...]


def _ssd_prompt(u, lw, nb, lp, l_valid):
    t = u.shape[0]
    nc = lp // CHUNK
    row = lambda b, i: b * nc + i
    full = lambda shape: pl.BlockSpec(shape, lambda b, i: (0,) * len(shape))
    return pl.pallas_call(
        functools.partial(_ssd_prompt_kernel, l_valid=l_valid),
        grid=(nb, nc),
        in_specs=[
            pl.BlockSpec((CHUNK, XBC_DIM), lambda b, i: (row(b, i), U_XBC // XBC_DIM)),
            pl.BlockSpec((CHUNK, SSM_DIM), lambda b, i: (row(b, i), U_Z // SSM_DIM)),
            pl.BlockSpec((CHUNK, LANE), lambda b, i: (row(b, i), U_DT // LANE)),
            full((SSM_CONV, XBC_DIM)), full((1, XBC_DIM)), full((1, LANE)), full((1, LANE)),
            full((1, SSM_DIM)), full((1, SSM_DIM)),
        ],
        out_specs=[
            pl.BlockSpec((CHUNK, SSM_DIM), lambda b, i: (row(b, i), 0)),
            pl.BlockSpec((None, SSM_DIM, SSM_STATE), lambda b, i: (b, 0, 0)),
        ],
        out_shape=[jax.ShapeDtypeStruct((t, SSM_DIM), _BF), jax.ShapeDtypeStruct((nb, SSM_DIM, SSM_STATE), _F32)],
        scratch_shapes=[
            pltpu.VMEM((CHUNK + SUBLANE, XBC_DIM), _F32), pltpu.VMEM((CHUNK, XBC_DIM), _F32),
            pltpu.VMEM((CHUNK, SSM_DIM), _F32), pltpu.VMEM((CHUNK, SSM_DIM), _F32),
            pltpu.VMEM((SSM_DIM, SSM_STATE), _F32),
        ],
        compiler_params=_cparams(("parallel", "arbitrary")),
        name="ssd_prompt",
    )(u, u, u, lw["scw"], lw["scb"], lw["dtb"], lw["a"], lw["dskip"], lw["sng"])


def _ssd_sample_kernel(xbc_ref, z_ref, dt_ref, left_ref, h0_ref, cw_ref, cb_ref, dtb_ref, a_ref, d_ref, ng_ref,
                       o_ref, hout_ref, xp_ref, xa_ref, y_ref, xw_ref, yo_ref):
    xp_ref[:, 0:SUBLANE, :] = left_ref[...]
    xp_ref[:, SUBLANE:2 * SUBLANE, :] = xbc_ref[...].reshape(REQ_BLK, SUBLANE, XBC_DIM)
    for c in range(XBC_DIM // LANE):
        cs = slice(c * LANE, (c + 1) * LANE)
        acc = jnp.zeros((REQ_BLK, SUBLANE, LANE), _F32) + cb_ref[:, cs][None]
        for k in range(SSM_CONV):
            off = SUBLANE - (SSM_CONV - 1) + k
            acc = acc + cw_ref[k:k + 1, cs][None] * xp_ref[:, off:off + SUBLANE, cs]
        xa_ref[:, cs] = _silu(acc).reshape(CHUNK, LANE)

    ri = lax.broadcasted_iota(jnp.int32, (CHUNK, CHUNK), 0)
    ci = lax.broadcasted_iota(jnp.int32, (CHUNK, CHUNK), 1)
    same = (ri >> 3) == (ci >> 3)
    mask = jnp.logical_and(same, ci <= ri)
    tmat = jnp.where(mask, 1.0, 0.0).astype(_BF)
    lastmat = jnp.where(same, 1.0, 0.0).astype(_BF)
    dt = _softplus(dt_ref[...] + dtb_ref[...])
    gw = SSM_DIM // SSM_GROUPS

    for r in range(REQ_BLK):
        rs = slice(r * SUBLANE, (r + 1) * SUBLANE)
        for g in range(SSM_GROUPS):
            cm = xa_ref[rs, SSM_DIM + (SSM_GROUPS + g) * SSM_STATE:SSM_DIM + (SSM_GROUPS + g + 1) * SSM_STATE]
            hb = h0_ref[r, g * gw:(g + 1) * gw, :].astype(_BF)
            yo_ref[rs, g * gw:(g + 1) * gw] = _dot_nt(cm.astype(_BF), hb)

    def hprev_fn(g, cm):
        return yo_ref[:, g * gw:(g + 1) * gw]

    acum, atot = _ssd_chunk(xa_ref, dt, a_ref[...], mask, tmat, hprev_fn, d_ref, y_ref, xw_ref, lastmat=lastmat)
    _ssd_gate_norm(y_ref[...], z_ref[...], ng_ref, o_ref)

    xw_t = xw_ref[...].T
    dec = jnp.exp(atot)
    lane_req = lax.broadcasted_iota(jnp.int32, (gw, CHUNK), 1) >> 3
    for g in range(SSM_GROUPS):
        bm = xa_ref[:, SSM_DIM + g * SSM_STATE:SSM_DIM + (g + 1) * SSM_STATE].astype(_BF)
        xg = xw_t[g * gw:(g + 1) * gw, :]
        for r in range(REQ_BLK):
            st = _dot(jnp.where(lane_req == r, xg, 0.0).astype(_BF), bm)
            for hh in range(SSM_HEADS // SSM_GROUPS):
                h = g * (SSM_HEADS // SSM_GROUPS) + hh
                hs = slice(h * SSM_HEAD_DIM, (h + 1) * SSM_HEAD_DIM)
                d = dec[r * SUBLANE:r * SUBLANE + 1, h:h + 1]
                hout_ref[r, hs, :] = h0_ref[r, hs, :] * d + st[hh * SSM_HEAD_DIM:(hh + 1) * SSM_HEAD_DIM, :]


def _ssd_sample(u, left8, h0_all, layer, lw):
    t = u.shape[0]
    nreq = t // SUBLANE
    full = lambda shape: pl.BlockSpec(shape, lambda i: (0,) * len(shape))
    return pl.pallas_call(
        _ssd_sample_kernel,
        grid=(nreq // REQ_BLK,),
        in_specs=[
            pl.BlockSpec((CHUNK, XBC_DIM), lambda i: (i, U_XBC // XBC_DIM)),
            pl.BlockSpec((CHUNK, SSM_DIM), lambda i: (i, U_Z // SSM_DIM)),
            pl.BlockSpec((CHUNK, LANE), lambda i: (i, U_DT // LANE)),
            pl.BlockSpec((REQ_BLK, SUBLANE, XBC_DIM), lambda i: (i, 0, 0)),
            pl.BlockSpec((None, REQ_BLK, SSM_DIM, SSM_STATE), lambda i: (layer, i, 0, 0)),
            full((SSM_CONV, XBC_DIM)), full((1, XBC_DIM)), full((1, LANE)), full((1, LANE)),
            full((1, SSM_DIM)), full((1, SSM_DIM)),
        ],
        out_specs=[
            pl.BlockSpec((CHUNK, SSM_DIM), lambda i: (i, 0)),
            pl.BlockSpec((REQ_BLK, SSM_DIM, SSM_STATE), lambda i: (i, 0, 0)),
        ],
        out_shape=[jax.ShapeDtypeStruct((t, SSM_DIM), _BF), jax.ShapeDtypeStruct((nreq, SSM_DIM, SSM_STATE), _F32)],
        scratch_shapes=[
            pltpu.VMEM((REQ_BLK, 2 * SUBLANE, XBC_DIM), _F32), pltpu.VMEM((CHUNK, XBC_DIM), _F32),
            pltpu.VMEM((CHUNK, SSM_DIM), _F32), pltpu.VMEM((CHUNK, SSM_DIM), _F32),
            pltpu.VMEM((CHUNK, SSM_DIM), _F32),
        ],
        compiler_params=_cparams(("parallel",)),
        name="ssd_sample",
    )(u, u, u, left8, h0_all, lw["scw"], lw["scb"], lw["dtb"], lw["a"], lw["dskip"], lw["sng"])


def _outproj_kernel(cy_ref, my_ref, sy_ref, h_ref, w_ref, o_ref, mix_ref):
    @pl.when(pl.program_id(1) == 0)
    def _():
        mix_ref[:, 0:CONV_DIM] = cy_ref[...].astype(_BF)
        mix_ref[:, CONV_DIM:CONV_DIM + MLA_DIM] = my_ref[...].astype(_BF)
        mix_ref[:, CONV_DIM + MLA_DIM:] = sy_ref[...].astype(_BF)

    o_ref[...] = h_ref[...] + _dot(mix_ref[...], w_ref[...])


def _outproj(cy, my, sy, h, w_all, layer, tm, tn=D_MODEL):
    t = h.shape[0]
    return pl.pallas_call(
        _outproj_kernel,
        grid=(t // tm, D_MODEL // tn),
        in_specs=[
            pl.BlockSpec((tm, CONV_DIM), lambda i, j: (i, 0)),
            pl.BlockSpec((tm, MLA_DIM), lambda i, j: (i, 0)),
            pl.BlockSpec((tm, SSM_DIM), lambda i, j: (i, 0)),
            pl.BlockSpec((tm, tn), lambda i, j: (i, j)),
            pl.BlockSpec((None, D_MODEL, tn), lambda i, j: (layer, 0, j)),
        ],
        out_specs=pl.BlockSpec((tm, tn), lambda i, j: (i, j)),
        out_shape=jax.ShapeDtypeStruct((t, D_MODEL), _F32),
        scratch_shapes=[pltpu.VMEM((tm, D_MODEL), _BF)],
        compiler_params=_cparams(("parallel", "arbitrary")),
        name="outproj",
    )(cy, my, sy, h, w_all)


def _final_norm_kernel(x_ref, g_ref, o_ref):
    x = x_ref[...]
    ms = jnp.mean(x * x, axis=-1, keepdims=True)
    o_ref[...] = x * lax.rsqrt(ms + EPS) * g_ref[...]


def _final_norm(h, g, tm):
    t = h.shape[0]
    return pl.pallas_call(
        _final_norm_kernel,
        grid=(t // tm,),
        in_specs=[pl.BlockSpec((tm, D_MODEL), lambda i: (i, 0)), pl.BlockSpec((1, D_MODEL), lambda i: (0, 0))],
        out_specs=pl.BlockSpec((tm, D_MODEL), lambda i: (i, 0)),
        out_shape=jax.ShapeDtypeStruct((t, D_MODEL), _F32),
        compiler_params=_cparams(("parallel",)),
        name="final_norm",
    )(h, g)


def _rot_half_cols(w):
    half = ROPE_DIM // 2
    return jnp.concatenate([-w[..., half:], w[..., :half]], axis=-1)


def _prep_weights(norm_g, w_in, conv_dw_w, conv_dw_b, conv_ln_g, conv_ln_b, conv_pw_w, conv_pw_b, mla_qa_g,
                  mla_wq_b, mla_kva_g, mla_wkv_b, ssm_conv_w, ssm_conv_b, ssm_dt_bias, ssm_A_log, ssm_D,
                  ssm_norm_g, w_out):
    w_in_t = jnp.swapaxes(w_in, 1, 2)
    o = 0
    parts = {}
    for name, width in (("glu", 2 * CONV_DIM), ("cg", CONV_DIM), ("cq", Q_LORA), ("ckv", KV_LORA),
                        ("kpe", ROPE_DIM), ("mg", MLA_DIM), ("z", SSM_DIM), ("xbc", XBC_DIM), ("dt", SSM_HEADS)):
        parts[name] = w_in_t[:, o:o + width, :]
        o += width
    half = ROPE_DIM // 2
    kpe_rot = jnp.concatenate([-parts["kpe"][:, half:], parts["kpe"][:, :half]], axis=1)
    dt_pad = jnp.pad(parts["dt"], ((0, 0), (0, LANE - SSM_HEADS), (0, 0)))
    w_in_p = jnp.concatenate(
        [parts["glu"], parts["mg"], parts["xbc"], parts["cg"], parts["cq"], parts["z"], parts["ckv"],
         parts["kpe"], kpe_rot, dt_pad], axis=1).astype(_BF)

    depth = w_in.shape[0]
    wq_nope = mla_wq_b[..., :NOPE_DIM].reshape(depth, Q_LORA, MLA_HEADS * NOPE_DIM)
    wq_rope = mla_wq_b[..., NOPE_DIM:]
    wq_slots = jnp.concatenate([wq_rope, _rot_half_cols(wq_rope)], axis=-1).reshape(depth, Q_LORA, MLA_HEADS * LANE)
    wq = jnp.concatenate([wq_nope, wq_slots], axis=-1).astype(_BF)
    wuk = jnp.transpose(mla_wkv_b[..., :NOPE_DIM], (0, 2, 3, 1)).astype(_BF)
    wqt = jnp.swapaxes(wq, 1, 2)
    wukt = jnp.transpose(mla_wkv_b[..., :NOPE_DIM], (0, 2, 1, 3)).astype(_BF)
    wuvt = jnp.transpose(mla_wkv_b[..., NOPE_DIM:], (0, 2, 3, 1)).astype(_BF)
    wuv_all = mla_wkv_b[..., NOPE_DIM:].reshape(depth, KV_LORA, MLA_DIM).astype(_BF)

    def row(x):
        return x[:, None, :].astype(_F32)

    pad_heads = lambda x: jnp.pad(x, ((0, 0), (0, LANE - SSM_HEADS)))
    layers = []
    for l in range(depth):
        layers.append({
            "norm_g": row(norm_g)[l],
            "dww": conv_dw_w[l], "dwb": row(conv_dw_b)[l], "lng": row(conv_ln_g)[l], "lnb": row(conv_ln_b)[l],
            "pww": conv_pw_w[l].astype(_BF), "pwb": row(conv_pw_b)[l],
            "qag": row(mla_qa_g)[l], "wq": wq[l], "wuk": wuk[l], "kvg": row(mla_kva_g)[l],
            "wqt": wqt[l], "wukt": wukt[l], "wuvt": wuvt[l], "wuv_all": wuv_all[l],
            "scw": ssm_conv_w[l], "scb": row(ssm_conv_b)[l],
            "dtb": row(pad_heads(ssm_dt_bias))[l], "a": row(pad_heads(-jnp.exp(ssm_A_log.astype(_F32))))[l],
            "dskip": row(jnp.repeat(ssm_D, SSM_HEAD_DIM, axis=-1))[l], "sng": row(ssm_norm_g)[l],
        })
    return layers, w_in_p, w_out.astype(_BF)


def _rope_table(pos):
    inv = ROPE_THETA ** (-jnp.arange(0, ROPE_DIM, 2, dtype=_F32) / ROPE_DIM)
    ang = pos.astype(_F32)[:, None] * inv[None, :]
    c, s = jnp.cos(ang), jnp.sin(ang)
    return jnp.concatenate([c, c, s, s], axis=-1)


def _largest_tile(t, cands):
    for c in cands:
        if t % c == 0:
            return c
    raise ValueError(f"no row tile for {t}")


def kernel(x_prompt, x_sample, cache_ckv, cache_kpe, page_table, state_conv, state_ssm_conv, state_ssm, meta_tokens, norm_g, w_in, conv_dw_w, conv_dw_b, conv_ln_g, conv_ln_b, conv_pw_w, conv_pw_b, mla_qa_g, mla_wq_b, mla_kva_g, mla_wkv_b, ssm_conv_w, ssm_conv_b, ssm_dt_bias, ssm_A_log, ssm_D, ssm_norm_g, w_out, final_g):
    nb, seq, _ = x_prompt.shape
    nreq, dseq, _ = x_sample.shape
    depth = w_in.shape[0]
    n_pages = page_table.shape[1]
    page_size = cache_ckv.shape[2]
    assert dseq == SUBLANE and nreq % REQ_BLK == 0 and page_size == LANE
    l_valid = N_META + seq
    lp = -(-l_valid // ATT_BLK) * ATT_BLK
    tp, ts = nb * lp, nreq * dseq

    layers, w_in_all, w_out_all = _prep_weights(
        norm_g, w_in, conv_dw_w, conv_dw_b, conv_ln_g, conv_ln_b, conv_pw_w, conv_pw_b, mla_qa_g, mla_wq_b,
        mla_kva_g, mla_wkv_b, ssm_conv_w, ssm_conv_b, ssm_dt_bias, ssm_A_log, ssm_D, ssm_norm_g, w_out)

    meta = jnp.broadcast_to(meta_tokens[None].astype(_F32), (nb, N_META, D_MODEL))
    hp = jnp.concatenate([meta, x_prompt, jnp.zeros((nb, lp - l_valid, D_MODEL), _F32)], axis=1).reshape(tp, D_MODEL)
    hs = x_sample.reshape(ts, D_MODEL)
    cs_p = jnp.tile(_rope_table(jnp.arange(lp, dtype=jnp.int32)), (nb, 1))
    cst_p = cs_p.T
    past_len = n_pages * page_size
    cs_s = jnp.tile(_rope_table(past_len + jnp.arange(dseq, dtype=jnp.int32)), (nreq, 1))
    pt_flat = page_table.reshape(-1).astype(jnp.int32)
    cache_kpe_t = jnp.swapaxes(cache_kpe, 2, 3)
    left_conv = jnp.pad(state_conv, ((0, 0), (0, 0), (2, 0), (0, 0)))
    left_ssm = jnp.pad(state_ssm_conv, ((0, 0), (0, 0), (SUBLANE - (SSM_CONV - 1), 0), (0, 0)))
    h0_all = state_ssm.reshape(depth, nreq, SSM_DIM, SSM_STATE)

    tm_p = _largest_tile(tp, (768, 512, 384, 256, 128))
    tm_s = _largest_tile(ts, (512, 256, 128))
    tq_p = _largest_tile(tp, (384, 256, 128))
    tq_s = _largest_tile(ts, (256, 128))

    st_p = [[] for _ in range(5)]
    st_s = [[] for _ in range(5)]
    for l in range(depth):
        lw = layers[l]
        u = _inproj(hp, lw["norm_g"], w_in_all, l, tm_p)
        v, cy = _conv_prompt(u, lw, nb, lp)
        qt, kcat, vt, ckv, kpe = _qkt(u, cs_p, cst_p, lw, nb, lp)
        my = _attn_prompt(qt, kcat, vt, u, lw["wuvt"], nb, lp)
        sy, hfin = _ssd_prompt(u, lw, nb, lp, l_valid)
        hp = _outproj(cy, my, sy, hp, w_out_all, l, tm_p)
        u3 = u.reshape(nb, lp, U_TOTAL)
        st_p[0].append(ckv.reshape(nb, lp, KV_LORA)[:, :l_valid])
        st_p[1].append(kpe.reshape(nb, lp, ROPE_DIM)[:, :l_valid])
        st_p[2].append(v.reshape(nb, lp, CONV_DIM)[:, l_valid - (CONV_WIDTH - 1):l_valid])
        st_p[3].append(u3[:, l_valid - (SSM_CONV - 1):l_valid, U_XBC:U_XBC + XBC_DIM])
        st_p[4].append(hfin.reshape(nb, SSM_HEADS, SSM_HEAD_DIM, SSM_STATE))
        u = _inproj(hs, lw["norm_g"], w_in_all, l, tm_s)
        cst, cy = _conv_sample(u, left_conv[l], lw)
        q, kcat, ckv, kpe = _qk(u, cs_s, lw, tq_s, _F32)
        my = _attn_sample(q, kcat, u, lw["wuv_all"], cache_ckv, cache_kpe_t, pt_flat, l, n_pages)
        sy, hnew = _ssd_sample(u, left_ssm[l], h0_all, l, lw)
        hs = _outproj(cy, my, sy, hs, w_out_all, l, tm_s)
        u3 = u.reshape(nreq, dseq, U_TOTAL)
        st_s[0].append(ckv.reshape(nreq, dseq, KV_LORA))
        st_s[1].append(kpe.reshape(nreq, dseq, ROPE_DIM))
        st_s[2].append(cst[:, 2:, :])
        st_s[3].append(u3[:, dseq - (SSM_CONV - 1):, U_XBC:U_XBC + XBC_DIM])
        st_s[4].append(hnew.reshape(nreq, SSM_HEADS, SSM_HEAD_DIM, SSM_STATE))

    y_prompt = _final_norm(hp, final_g[None, :].astype(_F32), tm_p).reshape(nb, lp, D_MODEL)[:, N_META:l_valid]
    y_sample = _final_norm(hs, final_g[None, :].astype(_F32), tm_s).reshape(nreq, dseq, D_MODEL)
    outs_p = [jnp.stack(s) for s in st_p]
    outs_s = [jnp.stack(s) for s in st_s]
    return (y_prompt, y_sample, *outs_p, *outs_s)
```
